```python
import jax, jax.numpy as jnp
from jax import lax
import numpy as np

D_MODEL = 1024
BATCH = 2
SEQ = 16384
DEPTH = 2

N_A_LAYERS = DEPTH // 2
N_B_LAYERS = DEPTH - N_A_LAYERS

LRU_WIDTH = D_MODEL
LRU_BLOCKS = 8
LRU_BLOCK_W = LRU_WIDTH // LRU_BLOCKS
CONV_WIDTH = 4
LRU_C = 8.0

HEAD_DIM = 128
Q_HEADS = D_MODEL // HEAD_DIM
KV_HEADS = 2
Q_PER_KV = Q_HEADS // KV_HEADS
DILATION_GROUPS = ((128, 1), (512, 4), (2048, 16))
N_GROUPS = len(DILATION_GROUPS)
ATT_BLOCK = 128
ROPE_DIM = HEAD_DIM // 4
ROPE_THETA = 500000.0

FFN_HIDDEN = -(-8 * D_MODEL // (3 * 256)) * 256
EPS = 1e-6
NEG_INF = -1e30

kernel_name = "yoco_rglru_dilated_swa_hybrid"


def rms_norm(x, g):
    x32 = x.astype(jnp.float32)
    y = x32 * lax.rsqrt(jnp.mean(x32 * x32, axis=-1, keepdims=True) + EPS)
    return (y * g.astype(jnp.float32)).astype(x.dtype)


def rope_tables(positions):
    inv_freq = ROPE_THETA ** (-jnp.arange(0, ROPE_DIM, 2, dtype=jnp.float32) / ROPE_DIM)
    ang = positions.astype(jnp.float32)[..., None] * inv_freq
    return jnp.cos(ang), jnp.sin(ang)


def apply_partial_rope(x, cos, sin):
    extra = x.ndim - 3
    shp = cos.shape[:2] + (1,) * extra + cos.shape[-1:]
    c, s = cos.reshape(shp), sin.reshape(shp)
    half = ROPE_DIM // 2
    xr = x[..., :ROPE_DIM].astype(jnp.float32)
    x1, x2 = xr[..., :half], xr[..., half:]
    rot = jnp.concatenate([x1 * c - x2 * s, x2 * c + x1 * s], axis=-1)
    return jnp.concatenate([rot.astype(x.dtype), x[..., ROPE_DIM:]], axis=-1)


def causal_depthwise_conv(x, w, b):
    y = lax.conv_general_dilated(
        x, w[:, None, :].astype(x.dtype), window_strides=(1,),
        padding=((CONV_WIDTH - 1, 0),), dimension_numbers=('NWC', 'WIO', 'NWC'),
        feature_group_count=x.shape[-1])
    return y + b.astype(x.dtype)


def rg_lru(x, w_a, b_a, w_x, b_x, lam):
    B, S, C = x.shape
    x32 = x.astype(jnp.float32)
    xb = x32.reshape(B, S, LRU_BLOCKS, LRU_BLOCK_W)
    r_gate = jax.nn.sigmoid(jnp.einsum('bshi,hij->bshj', xb, w_a.astype(jnp.float32)).reshape(B, S, C) + b_a.astype(jnp.float32))
    i_gate = jax.nn.sigmoid(jnp.einsum('bshi,hij->bshj', xb, w_x.astype(jnp.float32)).reshape(B, S, C) + b_x.astype(jnp.float32))
    log_a = -LRU_C * r_gate * jax.nn.softplus(-lam.astype(jnp.float32))
    a = jnp.exp(log_a)
    u = jnp.sqrt(-jnp.expm1(2.0 * log_a)) * (i_gate * x32)

    def combine(left, right):
        a1, b1 = left
        a2, b2 = right
        return a1 * a2, a2 * b1 + b2

    _, h = lax.associative_scan(combine, (a, u), axis=1)
    return h.astype(x.dtype)


def recurrent_mixer(h, w_in, conv_w, conv_b, ga_w, ga_b, gx_w, gx_b, lam, w_out):
    proj = h @ w_in
    y_branch, x_branch = jnp.split(proj, 2, axis=-1)
    gate = jax.nn.gelu(y_branch, approximate=True)
    xc = causal_depthwise_conv(x_branch, conv_w, conv_b)
    hr = rg_lru(xc, ga_w, ga_b, gx_w, gx_b, lam)
    return (gate * hr) @ w_out


def swiglu(h, w_in, w_out):
    g, u = jnp.split(h @ w_in, 2, axis=-1)
    return (jax.nn.silu(g) * u) @ w_out


def shared_kv(h, kv_norm, w_kv, k_norm, cos, sin):
    B, S, _ = h.shape
    hn = rms_norm(h, kv_norm)
    kv = (hn @ w_kv).reshape(B, S, N_GROUPS, 2, KV_HEADS, HEAD_DIM)
    k, v = kv[:, :, :, 0], kv[:, :, :, 1]
    k = rms_norm(k, k_norm[:, None, :])
    k = apply_partial_rope(k, cos, sin)
    return k, v


def dilated_window_attention(q, k, v, window, dilation):
    B, S = q.shape[:2]
    n_keys = window // dilation
    span = dilation * ATT_BLOCK
    S_pad = -(-S // span) * span
    L = S_pad // dilation
    nb = L // ATT_BLOCK
    pad = S_pad - S

    def to_phase(t):
        t = jnp.pad(t, [(0, 0), (0, pad)] + [(0, 0)] * (t.ndim - 2))
        t = t.reshape((B, L, dilation) + t.shape[2:])
        t = jnp.moveaxis(t, 2, 1)
        return t.reshape((B, dilation, nb, ATT_BLOCK) + t.shape[3:])

    def with_prev(t):
        prev = jnp.pad(t, [(0, 0), (0, 0), (1, 0)] + [(0, 0)] * (t.ndim - 3))[:, :, :-1]
        return jnp.concatenate([prev, t], axis=3)

    qp = to_phase(q)
    kb = with_prev(to_phase(k))
    vb = with_prev(to_phase(v))
    s = jnp.einsum('bpnqhgd,bpnkhd->bpnhgqk', qp, kb).astype(jnp.float32) * (HEAD_DIM ** -0.5)
    q_idx = jnp.arange(ATT_BLOCK)[:, None] + ATT_BLOCK
    k_idx = jnp.arange(2 * ATT_BLOCK)[None, :]
    dist = q_idx - k_idx
    band = (dist >= 0) & (dist <= n_keys)
    has_prev = (jnp.arange(nb)[:, None, None] > 0) | (k_idx >= ATT_BLOCK)[None]
    mask = band[None] & has_prev
    s = jnp.where(mask[None, None, :, None, None], s, NEG_INF)
    m = jnp.max(s, axis=-1, keepdims=True)
    p = jnp.exp(s - m)
    den = jnp.sum(p, axis=-1, keepdims=True)
    o = jnp.einsum('bpnhgqk,bpnkhd->bpnqhgd', (p / den).astype(v.dtype), vb)
    lse = jnp.moveaxis((m + jnp.log(den))[..., 0], -1, 3)

    def from_phase(t):
        t = t.reshape((B, dilation, L) + t.shape[4:])
        t = jnp.moveaxis(t, 1, 2).reshape((B, S_pad) + t.shape[3:])
        return t[:, :S]

    return from_phase(o), from_phase(lse)


def dilated_attention_mixer(h, w_q, q_norm, w_o, k, v, cos, sin):
    B, S, _ = h.shape
    q = (h @ w_q).reshape(B, S, N_GROUPS, Q_HEADS, HEAD_DIM)
    q = rms_norm(q, q_norm[:, None, :])
    q = apply_partial_rope(q, cos, sin)
    q = q.reshape(B, S, N_GROUPS, KV_HEADS, Q_PER_KV, HEAD_DIM)
    outs, lses = [], []
    for g, (window, dilation) in enumerate(DILATION_GROUPS):
        o, l = dilated_window_attention(q[:, :, g], k[:, :, g], v[:, :, g], window, dilation)
        outs.append(o)
        lses.append(l)
    wts = jax.nn.softmax(jnp.stack(lses, axis=0), axis=0)
    o = jnp.sum(wts[..., None] * jnp.stack(outs, axis=0).astype(jnp.float32), axis=0).astype(h.dtype)
    return o.reshape(B, S, Q_HEADS * HEAD_DIM) @ w_o


def setup_inputs(seed: int = 0) -> dict:
    key = jax.random.key(seed)
    ks = jax.random.split(key, 32)

    def nrm(k, shape, fan_in):
        return jax.random.normal(k, shape, jnp.float32) * (fan_in ** -0.5)

    def gain(k, shape):
        return 1.0 + 0.05 * jax.random.normal(k, shape, jnp.float32)

    def bias(k, shape):
        return 0.01 * jax.random.normal(k, shape, jnp.float32)

    nA, nB = N_A_LAYERS, N_B_LAYERS
    a8 = jax.random.uniform(ks[9], (nA, LRU_WIDTH), jnp.float32, 0.9, 0.999)
    a0 = a8 ** (1.0 / LRU_C)
    lam = jnp.log(a0) - jnp.log1p(-a0)
    return {
        "x": jax.random.normal(ks[0], (BATCH, SEQ, D_MODEL), jnp.float32),
        "positions": jnp.broadcast_to(jnp.arange(SEQ, dtype=jnp.int32)[None, :], (BATCH, SEQ)),
        "a_norm": gain(ks[1], (nA, D_MODEL)),
        "a_w_in": nrm(ks[2], (nA, D_MODEL, 2 * LRU_WIDTH), D_MODEL),
        "a_conv_w": nrm(ks[3], (nA, CONV_WIDTH, LRU_WIDTH), CONV_WIDTH),
        "a_conv_b": bias(ks[4], (nA, LRU_WIDTH)),
        "a_gate_a_w": nrm(ks[5], (nA, LRU_BLOCKS, LRU_BLOCK_W, LRU_BLOCK_W), LRU_BLOCK_W),
        "a_gate_a_b": bias(ks[6], (nA, LRU_WIDTH)),
        "a_gate_x_w": nrm(ks[7], (nA, LRU_BLOCKS, LRU_BLOCK_W, LRU_BLOCK_W), LRU_BLOCK_W),
        "a_gate_x_b": bias(ks[8], (nA, LRU_WIDTH)),
        "a_lambda": lam,
        "a_w_out": nrm(ks[10], (nA, LRU_WIDTH, D_MODEL), LRU_WIDTH),
        "a_ffn_norm": gain(ks[11], (nA, D_MODEL)),
        "a_ffn_w_in": nrm(ks[12], (nA, D_MODEL, 2 * FFN_HIDDEN), D_MODEL),
        "a_ffn_w_out": nrm(ks[13], (nA, FFN_HIDDEN, D_MODEL), FFN_HIDDEN),
        "kv_norm": gain(ks[14], (D_MODEL,)),
        "w_kv": nrm(ks[15], (D_MODEL, N_GROUPS * 2 * KV_HEADS * HEAD_DIM), D_MODEL),
        "k_norm": gain(ks[16], (N_GROUPS, HEAD_DIM)),
        "b_norm": gain(ks[17], (nB, D_MODEL)),
        "b_w_q": nrm(ks[18], (nB, D_MODEL, N_GROUPS * Q_HEADS * HEAD_DIM), D_MODEL),
        "b_q_norm": gain(ks[19], (nB, N_GROUPS, HEAD_DIM)),
        "b_w_o": nrm(ks[20], (nB, Q_HEADS * HEAD_DIM, D_MODEL), Q_HEADS * HEAD_DIM),
        "b_ffn_norm": gain(ks[21], (nB, D_MODEL)),
        "b_ffn_w_in": nrm(ks[22], (nB, D_MODEL, 2 * FFN_HIDDEN), D_MODEL),
        "b_ffn_w_out": nrm(ks[23], (nB, FFN_HIDDEN, D_MODEL), FFN_HIDDEN),
    }


def reference(x, positions, a_norm, a_w_in, a_conv_w, a_conv_b, a_gate_a_w, a_gate_a_b,
              a_gate_x_w, a_gate_x_b, a_lambda, a_w_out, a_ffn_norm, a_ffn_w_in, a_ffn_w_out,
              kv_norm, w_kv, k_norm, b_norm, b_w_q, b_q_norm, b_w_o, b_ffn_norm, b_ffn_w_in,
              b_ffn_w_out):
    cos, sin = rope_tables(positions)
    h = x
    k = v = None
    for layer in range(DEPTH):
        if layer < N_A_LAYERS:
            i = layer
            h = h + recurrent_mixer(rms_norm(h, a_norm[i]), a_w_in[i], a_conv_w[i], a_conv_b[i],
                                    a_gate_a_w[i], a_gate_a_b[i], a_gate_x_w[i], a_gate_x_b[i],
                                    a_lambda[i], a_w_out[i])
            h = h + swiglu(rms_norm(h, a_ffn_norm[i]), a_ffn_w_in[i], a_ffn_w_out[i])
        else:
            if layer == N_A_LAYERS:
                k, v = shared_kv(h, kv_norm, w_kv, k_norm, cos, sin)
            j = layer - N_A_LAYERS
            h = h + dilated_attention_mixer(rms_norm(h, b_norm[j]), b_w_q[j], b_q_norm[j], b_w_o[j],
                                            k, v, cos, sin)
            h = h + swiglu(rms_norm(h, b_ffn_norm[j]), b_ffn_w_in[j], b_ffn_w_out[j])
    return h
```

```python
import functools

import jax
import jax.numpy as jnp
from jax import lax
from jax.experimental import pallas as pl
from jax.experimental.pallas import tpu as pltpu

D_MODEL = 1024
LRU_BLOCKS = 8
LRU_BLOCK_W = D_MODEL // LRU_BLOCKS
CONV_WIDTH = 4
LRU_C = 8.0
HEAD_DIM = 128
Q_HEADS = D_MODEL // HEAD_DIM
KV_HEADS = 2
Q_PER_KV = Q_HEADS // KV_HEADS
DILATION_GROUPS = ((128, 1), (512, 4), (2048, 16))
N_GROUPS = len(DILATION_GROUPS)
ATT_BLOCK = 128
ROPE_DIM = HEAD_DIM // 4
ROPE_HALF = ROPE_DIM // 2
ROPE_THETA = 500000.0
EPS = 1e-6
NEG_INF = -1e30

LANES = 128
SUBLANES = 8
N_SLABS = D_MODEL // LANES
SPAN = ATT_BLOCK * max(d for _, d in DILATION_GROUPS)
PROJ_CHUNK = 512
MIXER_TILE = 512
FFN_TILE = 512
VMEM_LIMIT = 56 * 1024 * 1024

F32 = jnp.float32
BF16 = jnp.bfloat16


def _rms_rows(x):
    return x * lax.rsqrt(jnp.mean(x * x, axis=-1, keepdims=True) + EPS)


def _sigmoid(x):
    return 1.0 / (1.0 + jnp.exp(-x))


def _dot(a, b):
    return jnp.dot(a, b, preferred_element_type=F32)


def _mixer_kernel(h_ref, g_ref, win_ref, cw_ref, cb_ref, wg_ref, ba_ref, bx_ref, lam_ref,
                  wout_ref, o_ref, xpad_ref, a_ref, u_ref, hr_ref, carry_ref):
    j = pl.program_id(1)
    tm = h_ref.shape[0]
    x = h_ref[...]
    hn = (_rms_rows(x) * g_ref[...]).astype(BF16)
    proj = _dot(hn, win_ref[...])
    yb = proj[:, :D_MODEL]
    gate = 0.5 * yb * (1.0 + jnp.tanh(0.7978845608028654 * (yb + 0.044715 * (yb * yb * yb))))
    xb = proj[:, D_MODEL:]

    @pl.when(j == 0)
    def _():
        xpad_ref[0:SUBLANES, :] = jnp.zeros((SUBLANES, D_MODEL), F32)
        carry_ref[...] = jnp.zeros_like(carry_ref)

    @pl.when(j > 0)
    def _():
        xpad_ref[0:SUBLANES, :] = xpad_ref[tm:tm + SUBLANES, :]

    xpad_ref[SUBLANES:, :] = xb
    xc = cb_ref[...]
    for k in range(CONV_WIDTH):
        off = SUBLANES - (CONV_WIDTH - 1) + k
        xc = xc + cw_ref[k:k + 1, :] * xpad_ref[off:off + tm, :]

    xcb = xc.astype(BF16)
    ra, rx = [], []
    for i in range(LRU_BLOCKS):
        r = _dot(xcb[:, i * LRU_BLOCK_W:(i + 1) * LRU_BLOCK_W], wg_ref[i])
        ra.append(r[:, :LRU_BLOCK_W])
        rx.append(r[:, LRU_BLOCK_W:])
    r_gate = _sigmoid(jnp.concatenate(ra, axis=1) + ba_ref[...])
    i_gate = _sigmoid(jnp.concatenate(rx, axis=1) + bx_ref[...])
    z = -lam_ref[...]
    softplus = jnp.maximum(z, 0.0) + jnp.log1p(jnp.exp(-jnp.abs(z)))
    log_a = (-LRU_C) * r_gate * softplus
    a = jnp.exp(log_a)
    a_ref[...] = a
    u_ref[...] = jnp.sqrt(-jnp.tanh(log_a) * (1.0 + a * a)) * (i_gate * xc)

    row = lax.broadcasted_iota(jnp.int32, (SUBLANES, D_MODEL), 0)

    def group(gi, hc):
        off = pl.multiple_of(gi * SUBLANES, SUBLANES)
        a = a_ref[pl.ds(off, SUBLANES), :]
        u = u_ref[pl.ds(off, SUBLANES), :]
        for k in (1, 2, 4):
            keep = row >= k
            u = u + jnp.where(keep, a * pltpu.roll(u, k, 0), 0.0)
            a = a * jnp.where(keep, pltpu.roll(a, k, 0), 1.0)
        hg = u + a * hc
        hr_ref[pl.ds(off, SUBLANES), :] = hg
        return jnp.broadcast_to(hg[SUBLANES - 1:SUBLANES, :], (SUBLANES, D_MODEL))

    carry_ref[...] = lax.fori_loop(0, tm // SUBLANES, group, carry_ref[...])

    y = (gate * hr_ref[...]).astype(BF16)
    o_ref[...] = x + _dot(y, wout_ref[...])


def _const_spec(shape):
    nd = len(shape)
    return pl.BlockSpec(shape, lambda *_: (0,) * nd, pipeline_mode=pl.Buffered(1))


def _mixer(h, g, w_in, conv_w, conv_b, w_gates, b_a, b_x, lam, w_out):
    B, S, D = h.shape
    tm = MIXER_TILE
    row = lambda v: v.reshape(1, D)
    tile = pl.BlockSpec((None, tm, D), lambda b, j: (b, j, 0))
    return pl.pallas_call(
        _mixer_kernel,
        grid=(B, S // tm),
        in_specs=[tile, _const_spec((1, D)), _const_spec(w_in.shape), _const_spec(conv_w.shape),
                  _const_spec((1, D)), _const_spec(w_gates.shape), _const_spec((1, D)),
                  _const_spec((1, D)), _const_spec((1, D)), _const_spec(w_out.shape)],
        out_specs=tile,
        out_shape=jax.ShapeDtypeStruct(h.shape, F32),
        scratch_shapes=[pltpu.VMEM((tm + SUBLANES, D), F32), pltpu.VMEM((tm, D), F32),
                        pltpu.VMEM((tm, D), F32), pltpu.VMEM((tm, D), F32),
                        pltpu.VMEM((SUBLANES, D), F32)],
        compiler_params=pltpu.CompilerParams(
            dimension_semantics=("arbitrary", "arbitrary"), vmem_limit_bytes=VMEM_LIMIT),
        name="mixer",
    )(h, row(g), w_in, conv_w, row(conv_b), w_gates, row(b_a), row(b_x), row(lam), w_out)


def _ffn_kernel(*refs, with_proj):
    if with_proj:
        h_ref, att_ref, wo_ref, g_ref, wg_ref, wu_ref, wd_ref, o_ref = refs
        x = h_ref[...] + _dot(att_ref[...], wo_ref[...])
    else:
        h_ref, g_ref, wg_ref, wu_ref, wd_ref, o_ref = refs
        x = h_ref[...]
    hn = (_rms_rows(x) * g_ref[...]).astype(BF16)
    gg = _dot(hn, wg_ref[...])
    uu = _dot(hn, wu_ref[...])
    act = (gg * _sigmoid(gg) * uu).astype(BF16)
    o_ref[...] = x + _dot(act, wd_ref[...])


def _ffn(h, g, w_gate, w_up, w_down, att=None, w_o=None):
    B, S, D = h.shape
    T = B * S
    tm = FFN_TILE
    with_proj = att is not None
    tile = pl.BlockSpec((tm, D), lambda i: (i, 0))
    args = [h.reshape(T, D)]
    specs = [tile]
    if with_proj:
        args += [att.reshape(T, D), w_o]
        specs += [tile, _const_spec(w_o.shape)]
    args += [g.reshape(1, D), w_gate, w_up, w_down]
    specs += [_const_spec((1, D)), _const_spec(w_gate.shape), _const_spec(w_up.shape),
              _const_spec(w_down.shape)]
    out = pl.pallas_call(
        functools.partial(_ffn_kernel, with_proj=with_proj),
        grid=(T // tm,),
        in_specs=specs,
        out_specs=tile,
        out_shape=jax.ShapeDtypeStruct((T, D), F32),
        compiler_params=pltpu.CompilerParams(
            dimension_semantics=("arbitrary",), vmem_limit_bytes=VMEM_LIMIT),
        name="ffn_proj" if with_proj else "ffn",
    )(*args)
    return out.reshape(B, S, D)


def _rope_kernel(pos_ref, invf_ref, sign_ref, cos_ref, sin_ref):
    n = pos_ref.shape[-1]
    posb = jnp.broadcast_to(pos_ref[...], (LANES, n)).T
    ang = posb * invf_ref[...]
    cos_ref[...] = jnp.cos(ang)
    sin_ref[...] = jnp.sin(ang) * sign_ref[...]


def _rope_tables(positions):
    B, S = positions.shape
    inv_freq = ROPE_THETA ** (-jnp.arange(0, ROPE_DIM, 2, dtype=F32) / ROPE_DIM)
    invf = jnp.concatenate([inv_freq, inv_freq, jnp.zeros((LANES - ROPE_DIM,), F32)])[None]
    sign = jnp.concatenate([-jnp.ones((ROPE_HALF,), F32), jnp.ones((LANES - ROPE_HALF,), F32)])[None]
    pos = positions.astype(F32).reshape(B, 1, S)
    table = pl.BlockSpec((None, SPAN, LANES), lambda b, n: (b, n, 0))
    return pl.pallas_call(
        _rope_kernel,
        grid=(B, S // SPAN),
        in_specs=[pl.BlockSpec((None, 1, SPAN), lambda b, n: (b, 0, n)),
                  _const_spec((1, LANES)), _const_spec((1, LANES))],
        out_specs=[table, table],
        out_shape=[jax.ShapeDtypeStruct((B, S, LANES), F32)] * 2,
        compiler_params=pltpu.CompilerParams(dimension_semantics=("arbitrary", "arbitrary")),
        name="rope",
    )(pos, invf, sign)


def _phase_major_pieces(row0, nrows, dilation):
    per_phase = SPAN // dilation
    pieces = []
    r = row0
    while r < row0 + nrows:
        p, l0 = divmod(r, per_phase)
        n = min(row0 + nrows - r, per_phase - l0)
        pieces.append((l0 * dilation + p, n))
        r += n
    return pieces


def _gather_rows(ref, pieces, dilation):
    parts = []
    for start, n in pieces:
        idx = pl.ds(start, n, stride=dilation) if dilation > 1 else pl.ds(start, n)
        parts.append(ref[idx, :])
    return parts[0] if len(parts) == 1 else jnp.concatenate(parts, axis=0)


def _rope_rotate(y, cosv, sinv, low_half):
    partner = jnp.where(low_half, pltpu.roll(y, LANES - ROPE_HALF, 1), pltpu.roll(y, ROPE_HALF, 1))
    return y * cosv + partner * sinv


def _qkv_group(slab_refs, gq, gkv, wq_ref, wkv_ref, qn, kn, cos_ref, sin_ref, q_out, kv_out,
               dilation):
    low_half = lax.broadcasted_iota(jnp.int32, (PROJ_CHUNK, LANES), 1) < ROPE_HALF
    for c in range(SPAN // PROJ_CHUNK):
        rows = slice(c * PROJ_CHUNK, (c + 1) * PROJ_CHUNK)
        pieces = _phase_major_pieces(c * PROJ_CHUNK, PROJ_CHUNK, dilation)
        x = jnp.concatenate([_gather_rows(s, pieces, dilation) for s in slab_refs], axis=1)
        y = _rms_rows(x)
        cosv = _gather_rows(cos_ref, pieces, dilation)
        sinv = _gather_rows(sin_ref, pieces, dilation)
        q = _dot((y * gq).astype(BF16), wq_ref[...])
        for hd in range(Q_HEADS):
            cols = slice(hd * HEAD_DIM, (hd + 1) * HEAD_DIM)
            qh = _rope_rotate(_rms_rows(q[:, cols]) * qn, cosv, sinv, low_half)
            q_out[rows, cols] = qh.astype(BF16)
        kv = _dot((y * gkv).astype(BF16), wkv_ref[...])
        for hd in range(KV_HEADS):
            cols = slice(hd * HEAD_DIM, (hd + 1) * HEAD_DIM)
            kh = _rope_rotate(_rms_rows(kv[:, cols]) * kn, cosv, sinv, low_half)
            kv_out[rows, cols] = kh.astype(BF16)
        vcols = slice(KV_HEADS * HEAD_DIM, 2 * KV_HEADS * HEAD_DIM)
        kv_out[rows, vcols] = kv[:, vcols].astype(BF16)


def _qkv_kernel(*refs):
    slab_refs = refs[:N_SLABS]
    gq_ref, gkv_ref, wq_ref, wkv_ref, qn_ref, kn_ref, cos_ref, sin_ref, q_out, kv_out = refs[N_SLABS:]
    g = pl.program_id(2)
    for gi, (_, dilation) in enumerate(DILATION_GROUPS):
        @pl.when(g == gi)
        def _(gi=gi, dilation=dilation):
            qn = qn_ref[gi:gi + 1, :] * (HEAD_DIM ** -0.5)
            kn = kn_ref[gi:gi + 1, :]
            _qkv_group(slab_refs, gq_ref[...], gkv_ref[...], wq_ref, wkv_ref, qn, kn,
                       cos_ref, sin_ref, q_out, kv_out, dilation)


def _qkv(h, g_q, g_kv, w_q, w_kv, q_norm, k_norm, cos_t, sin_t):
    B, S, D = h.shape
    qw = Q_HEADS * HEAD_DIM
    kvw = 2 * KV_HEADS * HEAD_DIM
    slab = lambda c: pl.BlockSpec((None, SPAN, LANES), lambda b, n, g: (b, n, c))
    table = pl.BlockSpec((None, SPAN, LANES), lambda b, n, g: (b, n, 0))
    const = lambda shape: pl.BlockSpec(shape, lambda b, n, g: (0,) * len(shape))
    return pl.pallas_call(
        _qkv_kernel,
        grid=(B, S // SPAN, N_GROUPS),
        in_specs=[slab(c) for c in range(N_SLABS)] + [
            const((1, D)), const((1, D)),
            pl.BlockSpec((D, qw), lambda b, n, g: (0, g)),
            pl.BlockSpec((D, kvw), lambda b, n, g: (0, g)),
            const(q_norm.shape), const(k_norm.shape), table, table],
        out_specs=[pl.BlockSpec((None, SPAN, qw), lambda b, n, g: (b, n, g)),
                   pl.BlockSpec((None, SPAN, kvw), lambda b, n, g: (b, n, g))],
        out_shape=[jax.ShapeDtypeStruct((B, S, N_GROUPS * qw), BF16),
                   jax.ShapeDtypeStruct((B, S, N_GROUPS * kvw), BF16)],
        compiler_params=pltpu.CompilerParams(
            dimension_semantics=("arbitrary",) * 3, vmem_limit_bytes=VMEM_LIMIT),
        name="qkv",
    )(*([h] * N_SLABS), g_q.reshape(1, D), g_kv.reshape(1, D), w_q, w_kv, q_norm, k_norm,
      cos_t, sin_t)


def _attn_kernel(*refs):
    q_refs = refs[0:N_GROUPS]
    kv_refs = refs[N_GROUPS:5 * N_GROUPS]
    out_ref, o_scr, l_scr, bias_scr = refs[5 * N_GROUPS:]
    first_span = pl.program_id(1) == 0
    rows = Q_PER_KV * ATT_BLOCK

    qi = lax.broadcasted_iota(jnp.int32, (rows, 2 * ATT_BLOCK), 0) % ATT_BLOCK
    ki = lax.broadcasted_iota(jnp.int32, (rows, 2 * ATT_BLOCK), 1)
    vis_cur = jnp.logical_and(ki >= ATT_BLOCK, ki - ATT_BLOCK <= qi)
    vis_prev = jnp.logical_and(ki < ATT_BLOCK, ki >= qi)
    bias_scr[0] = jnp.where(jnp.logical_or(vis_cur, vis_prev), 0.0, NEG_INF)
    bias_scr[1] = jnp.where(vis_cur, 0.0, NEG_INF)

    for gi in reversed(range(N_GROUPS)):
        dilation = DILATION_GROUPS[gi][1]
        per_phase = SPAN // dilation
        nblk = per_phase // ATT_BLOCK
        q_ref = q_refs[gi]
        kc_ref, kp_ref, vc_ref, vp_ref = kv_refs[4 * gi:4 * gi + 4]

        def block(blk, carry, gi=gi, dilation=dilation, per_phase=per_phase, nblk=nblk,
                  q_ref=q_ref, kc_ref=kc_ref, kp_ref=kp_ref, vc_ref=vc_ref, vp_ref=vp_ref):
            r0 = pl.multiple_of(blk * ATT_BLOCK, ATT_BLOCK)
            j = blk % nblk
            phase = blk // nblk
            head_of_phase = j == 0
            q4 = q_ref[pl.ds(r0, ATT_BLOCK), :]
            qs = jnp.concatenate(
                [q4[:, h * HEAD_DIM:(h + 1) * HEAD_DIM] for h in range(Q_PER_KV)], axis=0)
            r_in = pl.multiple_of(jnp.maximum(r0 - ATT_BLOCK, 0), ATT_BLOCK)
            r_out = pl.multiple_of(
                jnp.where(head_of_phase, r0 + per_phase - ATT_BLOCK, 0), ATT_BLOCK)
            k_prev = jnp.where(head_of_phase, kp_ref[pl.ds(r_out, ATT_BLOCK), :],
                               kc_ref[pl.ds(r_in, ATT_BLOCK), :])
            v_prev = jnp.where(head_of_phase, vp_ref[pl.ds(r_out, ATT_BLOCK), :],
                               vc_ref[pl.ds(r_in, ATT_BLOCK), :])
            kcat = jnp.concatenate([k_prev, kc_ref[pl.ds(r0, ATT_BLOCK), :]], axis=0)
            vcat = jnp.concatenate([v_prev, vc_ref[pl.ds(r0, ATT_BLOCK), :]], axis=0)
            s = lax.dot_general(qs, kcat, (((1,), (1,)), ((), ())), preferred_element_type=F32)
            no_prev = jnp.logical_and(head_of_phase, first_span)
            s = s + jnp.where(no_prev, bias_scr[1], bias_scr[0])
            m = jnp.max(s, axis=-1, keepdims=True)
            p = jnp.exp(s - m)
            den = jnp.sum(p, axis=-1, keepdims=True)
            o = _dot(p.astype(BF16), vcat) * (1.0 / den)
            lse = m + jnp.log(den)
            t0 = j * (ATT_BLOCK * dilation) + phase
            for h in range(Q_PER_KV):
                hrows = slice(h * ATT_BLOCK, (h + 1) * ATT_BLOCK)
                oh = o[hrows]
                lh = jnp.broadcast_to(lse[hrows], (ATT_BLOCK, HEAD_DIM))
                if gi > 0:
                    tok = pl.ds(t0, ATT_BLOCK, stride=dilation)
                    o_scr[gi - 1, h, tok, :] = oh
                    l_scr[gi - 1, h, tok, :] = lh
                else:
                    tok = pl.ds(r0, ATT_BLOCK)
                    os_, ls_ = [oh], [lh]
                    for og in range(N_GROUPS - 1):
                        os_.append(o_scr[og, h, tok, :])
                        ls_.append(l_scr[og, h, tok, :])
                    top = functools.reduce(jnp.maximum, ls_)
                    ws = [jnp.exp(l - top) for l in ls_]
                    num = functools.reduce(lambda a, b: a + b, [w * v for w, v in zip(ws, os_)])
                    merged = num / functools.reduce(lambda a, b: a + b, ws)
                    out_ref[tok, h * HEAD_DIM:(h + 1) * HEAD_DIM] = merged.astype(BF16)
            return carry

        lax.fori_loop(0, SPAN // ATT_BLOCK, block, 0)


def _attention(q, kv):
    B, S, _ = q.shape
    qw = Q_PER_KV * HEAD_DIM
    prev = lambda n: jnp.maximum(n - 1, 0)
    in_specs = []
    for gi in range(N_GROUPS):
        in_specs.append(pl.BlockSpec((None, SPAN, qw), lambda b, n, h, gi=gi: (b, n, gi * KV_HEADS + h)))
    for gi in range(N_GROUPS):
        kcol = lambda h, gi=gi: gi * 2 * KV_HEADS + h
        vcol = lambda h, gi=gi: gi * 2 * KV_HEADS + KV_HEADS + h
        for col in (kcol, vcol):
            in_specs.append(pl.BlockSpec((None, SPAN, HEAD_DIM),
                                         lambda b, n, h, col=col: (b, n, col(h))))
            in_specs.append(pl.BlockSpec((None, SPAN, HEAD_DIM),
                                         lambda b, n, h, col=col: (b, prev(n), col(h))))
    return pl.pallas_call(
        _attn_kernel,
        grid=(B, S // SPAN, KV_HEADS),
        in_specs=in_specs,
        out_specs=pl.BlockSpec((None, SPAN, qw), lambda b, n, h: (b, n, h)),
        out_shape=jax.ShapeDtypeStruct((B, S, Q_HEADS * HEAD_DIM), BF16),
        scratch_shapes=[pltpu.VMEM((N_GROUPS - 1, Q_PER_KV, SPAN, HEAD_DIM), F32),
                        pltpu.VMEM((N_GROUPS - 1, Q_PER_KV, SPAN, HEAD_DIM), F32),
                        pltpu.VMEM((2, Q_PER_KV * ATT_BLOCK, 2 * ATT_BLOCK), F32)],
        compiler_params=pltpu.CompilerParams(
            dimension_semantics=("arbitrary",) * 3, vmem_limit_bytes=VMEM_LIMIT),
        name="attn",
    )(*([q] * N_GROUPS), *([kv] * (4 * N_GROUPS)))


def kernel(x, positions, a_norm, a_w_in, a_conv_w, a_conv_b, a_gate_a_w, a_gate_a_b, a_gate_x_w, a_gate_x_b, a_lambda, a_w_out, a_ffn_norm, a_ffn_w_in, a_ffn_w_out, kv_norm, w_kv, k_norm, b_norm, b_w_q, b_q_norm, b_w_o, b_ffn_norm, b_ffn_w_in, b_ffn_w_out):
    assert x.shape[1] % SPAN == 0 and x.shape[2] == D_MODEL
    assert a_norm.shape[0] == 1 and b_norm.shape[0] == 1
    hidden = a_ffn_w_out.shape[1]
    h = x
    w_gates = jnp.concatenate([a_gate_a_w[0], a_gate_x_w[0]], axis=-1).astype(BF16)
    h = _mixer(h, a_norm[0], a_w_in[0].astype(BF16), a_conv_w[0], a_conv_b[0], w_gates,
               a_gate_a_b[0], a_gate_x_b[0], a_lambda[0], a_w_out[0].astype(BF16))
    h = _ffn(h, a_ffn_norm[0], a_ffn_w_in[0][:, :hidden].astype(BF16),
             a_ffn_w_in[0][:, hidden:].astype(BF16), a_ffn_w_out[0].astype(BF16))
    cos_t, sin_t = _rope_tables(positions)
    q, kv = _qkv(h, b_norm[0], kv_norm, b_w_q[0].astype(BF16), w_kv.astype(BF16),
                 b_q_norm[0], k_norm, cos_t, sin_t)
    att = _attention(q, kv)
    h = _ffn(h, b_ffn_norm[0], b_ffn_w_in[0][:, :hidden].astype(BF16),
             b_ffn_w_in[0][:, hidden:].astype(BF16), b_ffn_w_out[0].astype(BF16),
             att=att, w_o=b_w_o[0].astype(BF16))
    return h
```

```python
import functools

import jax
import jax.numpy as jnp
from jax import lax
from jax.experimental import pallas as pl
from jax.experimental.pallas import tpu as pltpu

D_MODEL = 1024
LRU_BLOCKS = 8
LRU_BLOCK_W = D_MODEL // LRU_BLOCKS
CONV_WIDTH = 4
LRU_C = 8.0
HEAD_DIM = 128
Q_HEADS = D_MODEL // HEAD_DIM
KV_HEADS = 2
Q_PER_KV = Q_HEADS // KV_HEADS
DILATION_GROUPS = ((128, 1), (512, 4), (2048, 16))
N_GROUPS = len(DILATION_GROUPS)
ATT_BLOCK = 128
ROPE_DIM = HEAD_DIM // 4
ROPE_HALF = ROPE_DIM // 2
ROPE_THETA = 500000.0
EPS = 1e-6
NEG_INF = -1e30

LANES = 128
SUBLANES = 8
N_SLABS = D_MODEL // LANES
SPAN = ATT_BLOCK * max(d for _, d in DILATION_GROUPS)
PROJ_CHUNK = 256
HEAD_PAIR = 2 * HEAD_DIM
assert KV_HEADS * HEAD_DIM == HEAD_PAIR
MIXER_TILE = 512
FFN_TILE = 512
VMEM_LIMIT = 56 * 1024 * 1024

F32 = jnp.float32
BF16 = jnp.bfloat16


def _rms_rows(x):
    return x * lax.rsqrt(jnp.mean(x * x, axis=-1, keepdims=True) + EPS)


def _sigmoid(x):
    return 1.0 / (1.0 + jnp.exp(-x))


def _dot(a, b):
    return jnp.dot(a, b, preferred_element_type=F32)


def _mixer_kernel(h_ref, g_ref, win_ref, cw_ref, cb_ref, wg_ref, ba_ref, bx_ref, lam_ref,
                  wout_ref, o_ref, xpad_ref, a_ref, u_ref, hr_ref, carry_ref):
    j = pl.program_id(1)
    tm = h_ref.shape[0]
    x = h_ref[...]
    hn = (_rms_rows(x) * g_ref[...]).astype(BF16)
    proj = _dot(hn, win_ref[...])
    yb = proj[:, :D_MODEL]
    gate = 0.5 * yb * (1.0 + jnp.tanh(0.7978845608028654 * (yb + 0.044715 * (yb * yb * yb))))
    xb = proj[:, D_MODEL:]

    @pl.when(j == 0)
    def _():
        xpad_ref[0:SUBLANES, :] = jnp.zeros((SUBLANES, D_MODEL), F32)
        carry_ref[...] = jnp.zeros_like(carry_ref)

    @pl.when(j > 0)
    def _():
        xpad_ref[0:SUBLANES, :] = xpad_ref[tm:tm + SUBLANES, :]

    xpad_ref[SUBLANES:, :] = xb
    xc = cb_ref[...]
    for k in range(CONV_WIDTH):
        off = SUBLANES - (CONV_WIDTH - 1) + k
        xc = xc + cw_ref[k:k + 1, :] * xpad_ref[off:off + tm, :]

    xcb = xc.astype(BF16)
    ra, rx = [], []
    for i in range(LRU_BLOCKS):
        r = _dot(xcb[:, i * LRU_BLOCK_W:(i + 1) * LRU_BLOCK_W], wg_ref[i])
        ra.append(r[:, :LRU_BLOCK_W])
        rx.append(r[:, LRU_BLOCK_W:])
    r_gate = _sigmoid(jnp.concatenate(ra, axis=1) + ba_ref[...])
    i_gate = _sigmoid(jnp.concatenate(rx, axis=1) + bx_ref[...])
    z = -lam_ref[...]
    softplus = jnp.maximum(z, 0.0) + jnp.log1p(jnp.exp(-jnp.abs(z)))
    log_a = (-LRU_C) * r_gate * softplus
    a = jnp.exp(log_a)
    a_ref[...] = a
    u_ref[...] = jnp.sqrt(-jnp.tanh(log_a) * (1.0 + a * a)) * (i_gate * xc)

    row = lax.broadcasted_iota(jnp.int32, (SUBLANES, D_MODEL), 0)

    def group(gi, hc):
        off = pl.multiple_of(gi * SUBLANES, SUBLANES)
        a = a_ref[pl.ds(off, SUBLANES), :]
        u = u_ref[pl.ds(off, SUBLANES), :]
        for k in (1, 2, 4):
            keep = row >= k
            u = u + jnp.where(keep, a * pltpu.roll(u, k, 0), 0.0)
            a = a * jnp.where(keep, pltpu.roll(a, k, 0), 1.0)
        hg = u + a * hc
        hr_ref[pl.ds(off, SUBLANES), :] = hg
        return jnp.broadcast_to(hg[SUBLANES - 1:SUBLANES, :], (SUBLANES, D_MODEL))

    carry_ref[...] = lax.fori_loop(0, tm // SUBLANES, group, carry_ref[...])

    y = (gate * hr_ref[...]).astype(BF16)
    o_ref[...] = x + _dot(y, wout_ref[...])


def _const_spec(shape):
    nd = len(shape)
    return pl.BlockSpec(shape, lambda *_: (0,) * nd, pipeline_mode=pl.Buffered(1))


def _mixer(h, g, w_in, conv_w, conv_b, w_gates, b_a, b_x, lam, w_out):
    B, S, D = h.shape
    tm = MIXER_TILE
    row = lambda v: v.reshape(1, D)
    tile = pl.BlockSpec((None, tm, D), lambda b, j: (b, j, 0))
    return pl.pallas_call(
        _mixer_kernel,
        grid=(B, S // tm),
        in_specs=[tile, _const_spec((1, D)), _const_spec(w_in.shape), _const_spec(conv_w.shape),
                  _const_spec((1, D)), _const_spec(w_gates.shape), _const_spec((1, D)),
                  _const_spec((1, D)), _const_spec((1, D)), _const_spec(w_out.shape)],
        out_specs=tile,
        out_shape=jax.ShapeDtypeStruct(h.shape, F32),
        scratch_shapes=[pltpu.VMEM((tm + SUBLANES, D), F32), pltpu.VMEM((tm, D), F32),
                        pltpu.VMEM((tm, D), F32), pltpu.VMEM((tm, D), F32),
                        pltpu.VMEM((SUBLANES, D), F32)],
        compiler_params=pltpu.CompilerParams(
            dimension_semantics=("arbitrary", "arbitrary"), vmem_limit_bytes=VMEM_LIMIT),
        name="mixer",
    )(h, row(g), w_in, conv_w, row(conv_b), w_gates, row(b_a), row(b_x), row(lam), w_out)


def _ffn_kernel(*refs, with_proj):
    if with_proj:
        h_ref, att_ref, wo_ref, g_ref, wg_ref, wu_ref, wd_ref, o_ref = refs
        x = h_ref[...] + _dot(att_ref[...], wo_ref[...])
    else:
        h_ref, g_ref, wg_ref, wu_ref, wd_ref, o_ref = refs
        x = h_ref[...]
    hn = (_rms_rows(x) * g_ref[...]).astype(BF16)
    gg = _dot(hn, wg_ref[...])
    uu = _dot(hn, wu_ref[...])
    act = (gg * _sigmoid(gg) * uu).astype(BF16)
    o_ref[...] = x + _dot(act, wd_ref[...])


def _ffn(h, g, w_gate, w_up, w_down, att=None, w_o=None):
    B, S, D = h.shape
    T = B * S
    tm = FFN_TILE
    with_proj = att is not None
    tile = pl.BlockSpec((tm, D), lambda i: (i, 0))
    args = [h.reshape(T, D)]
    specs = [tile]
    if with_proj:
        args += [att.reshape(T, D), w_o]
        specs += [tile, _const_spec(w_o.shape)]
    args += [g.reshape(1, D), w_gate, w_up, w_down]
    specs += [_const_spec((1, D)), _const_spec(w_gate.shape), _const_spec(w_up.shape),
              _const_spec(w_down.shape)]
    out = pl.pallas_call(
        functools.partial(_ffn_kernel, with_proj=with_proj),
        grid=(T // tm,),
        in_specs=specs,
        out_specs=tile,
        out_shape=jax.ShapeDtypeStruct((T, D), F32),
        compiler_params=pltpu.CompilerParams(
            dimension_semantics=("arbitrary",), vmem_limit_bytes=VMEM_LIMIT),
        name="ffn_proj" if with_proj else "ffn",
    )(*args)
    return out.reshape(B, S, D)


def _rope_kernel(pos_ref, invf_ref, sign_ref, cos_ref, sin_ref):
    n = pos_ref.shape[-1]
    posb = jnp.broadcast_to(pos_ref[...], (LANES, n)).T
    ang = posb * invf_ref[...]
    cos_ref[...] = jnp.cos(ang)
    sin_ref[...] = jnp.sin(ang) * sign_ref[...]


def _rope_tables(positions):
    B, S = positions.shape
    inv_freq = ROPE_THETA ** (-jnp.arange(0, ROPE_DIM, 2, dtype=F32) / ROPE_DIM)
    invf = jnp.concatenate([inv_freq, inv_freq, jnp.zeros((LANES - ROPE_DIM,), F32)])[None]
    sign = jnp.concatenate([-jnp.ones((ROPE_HALF,), F32), jnp.ones((LANES - ROPE_HALF,), F32)])[None]
    pos = positions.astype(F32).reshape(B, 1, S)
    table = pl.BlockSpec((None, SPAN, LANES), lambda b, n: (b, n, 0))
    return pl.pallas_call(
        _rope_kernel,
        grid=(B, S // SPAN),
        in_specs=[pl.BlockSpec((None, 1, SPAN), lambda b, n: (b, 0, n)),
                  _const_spec((1, LANES)), _const_spec((1, LANES))],
        out_specs=[table, table],
        out_shape=[jax.ShapeDtypeStruct((B, S, LANES), F32)] * 2,
        compiler_params=pltpu.CompilerParams(dimension_semantics=("arbitrary", "arbitrary")),
        name="rope",
    )(pos, invf, sign)


def _pair_matrices():
    shape = (HEAD_PAIR, HEAD_PAIR)
    src = lax.broadcasted_iota(jnp.int32, shape, 0)
    dst = lax.broadcasted_iota(jnp.int32, shape, 1)
    same_head = (src // HEAD_DIM) == (dst // HEAD_DIM)
    s, d = src % HEAD_DIM, dst % HEAD_DIM
    partner = jnp.where(d < ROPE_HALF, d + ROPE_HALF, jnp.where(d < ROPE_DIM, d - ROPE_HALF, -1))
    ones = jnp.where(same_head, 1.0, 0.0).astype(BF16)
    rot = jnp.where(jnp.logical_and(same_head, s == partner), 1.0, 0.0).astype(BF16)
    return ones, rot


def _norm_rope_pair(t, gain2, cos2, sin2, ones, rot):
    ss = _dot((t * t).astype(BF16), ones)
    z = t * gain2
    partner = _dot(z.astype(BF16), rot)
    return lax.rsqrt(ss * (1.0 / HEAD_DIM) + EPS) * (z * cos2 + partner * sin2)


def _qkv_group(slab_refs, gq, gkv, wq_ref, wkv_ref, qn2, kn2, cos_ref, sin_ref, q_out, kv_out,
               dilation):
    per_phase = SPAN // dilation
    n_pieces = max(1, PROJ_CHUNK // per_phase)
    piece = PROJ_CHUNK // n_pieces
    ones, rot = _pair_matrices()

    def chunk(c, carry):
        row0 = pl.multiple_of(c * PROJ_CHUNK, PROJ_CHUNK)

        def gather(ref):
            parts = []
            for i in range(n_pieces):
                r = row0 + i * piece
                if dilation == 1:
                    idx = pl.ds(r, piece)
                else:
                    idx = pl.ds((r % per_phase) * dilation + r // per_phase, piece, stride=dilation)
                parts.append(ref[idx, :])
            return parts[0] if n_pieces == 1 else jnp.concatenate(parts, axis=0)

        y = _rms_rows(jnp.concatenate([gather(s) for s in slab_refs], axis=1))
        cosv, sinv = gather(cos_ref), gather(sin_ref)
        cos2 = jnp.concatenate([cosv, cosv], axis=1)
        sin2 = jnp.concatenate([sinv, sinv], axis=1)
        rows = pl.ds(row0, PROJ_CHUNK)
        q = _dot((y * gq).astype(BF16), wq_ref[...])
        for i in range(Q_HEADS * HEAD_DIM // HEAD_PAIR):
            cols = slice(i * HEAD_PAIR, (i + 1) * HEAD_PAIR)
            q_out[rows, cols] = _norm_rope_pair(q[:, cols], qn2, cos2, sin2, ones, rot).astype(BF16)
        kv = _dot((y * gkv).astype(BF16), wkv_ref[...])
        kcols = slice(0, HEAD_PAIR)
        kv_out[rows, kcols] = _norm_rope_pair(kv[:, kcols], kn2, cos2, sin2, ones, rot).astype(BF16)
        vcols = slice(HEAD_PAIR, 2 * HEAD_PAIR)
        kv_out[rows, vcols] = kv[:, vcols].astype(BF16)
        return carry

    lax.fori_loop(0, SPAN // PROJ_CHUNK, chunk, 0)


def _qkv_kernel(*refs):
    slab_refs = refs[:N_SLABS]
    gq_ref, gkv_ref, wq_ref, wkv_ref, qn_ref, kn_ref, cos_ref, sin_ref, q_out, kv_out = refs[N_SLABS:]
    g = pl.program_id(2)
    for gi, (_, dilation) in enumerate(DILATION_GROUPS):
        @pl.when(g == gi)
        def _(gi=gi, dilation=dilation):
            qn = qn_ref[gi:gi + 1, :] * (HEAD_DIM ** -0.5)
            kn = kn_ref[gi:gi + 1, :]
            _qkv_group(slab_refs, gq_ref[...], gkv_ref[...], wq_ref, wkv_ref,
                       jnp.concatenate([qn, qn], axis=1), jnp.concatenate([kn, kn], axis=1),
                       cos_ref, sin_ref, q_out, kv_out, dilation)


def _qkv(h, g_q, g_kv, w_q, w_kv, q_norm, k_norm, cos_t, sin_t):
    B, S, D = h.shape
    qw = Q_HEADS * HEAD_DIM
    kvw = 2 * KV_HEADS * HEAD_DIM
    slab = lambda c: pl.BlockSpec((None, SPAN, LANES), lambda b, n, g: (b, n, c))
    table = pl.BlockSpec((None, SPAN, LANES), lambda b, n, g: (b, n, 0))
    const = lambda shape: pl.BlockSpec(shape, lambda b, n, g: (0,) * len(shape))
    return pl.pallas_call(
        _qkv_kernel,
        grid=(B, S // SPAN, N_GROUPS),
        in_specs=[slab(c) for c in range(N_SLABS)] + [
            const((1, D)), const((1, D)),
            pl.BlockSpec((D, qw), lambda b, n, g: (0, g)),
            pl.BlockSpec((D, kvw), lambda b, n, g: (0, g)),
            const(q_norm.shape), const(k_norm.shape), table, table],
        out_specs=[pl.BlockSpec((None, SPAN, qw), lambda b, n, g: (b, n, g)),
                   pl.BlockSpec((None, SPAN, kvw), lambda b, n, g: (b, n, g))],
        out_shape=[jax.ShapeDtypeStruct((B, S, N_GROUPS * qw), BF16),
                   jax.ShapeDtypeStruct((B, S, N_GROUPS * kvw), BF16)],
        compiler_params=pltpu.CompilerParams(
            dimension_semantics=("arbitrary",) * 3, vmem_limit_bytes=VMEM_LIMIT),
        name="qkv",
    )(*([h] * N_SLABS), g_q.reshape(1, D), g_kv.reshape(1, D), w_q, w_kv, q_norm, k_norm,
      cos_t, sin_t)


def _attn_kernel(*refs):
    q_refs = refs[0:N_GROUPS]
    kv_refs = refs[N_GROUPS:5 * N_GROUPS]
    out_ref, o_scr, l_scr, bias_scr = refs[5 * N_GROUPS:]
    first_span = pl.program_id(1) == 0
    rows = Q_PER_KV * ATT_BLOCK

    qi = lax.broadcasted_iota(jnp.int32, (rows, 2 * ATT_BLOCK), 0) % ATT_BLOCK
    ki = lax.broadcasted_iota(jnp.int32, (rows, 2 * ATT_BLOCK), 1)
    vis_cur = jnp.logical_and(ki >= ATT_BLOCK, ki - ATT_BLOCK <= qi)
    vis_prev = jnp.logical_and(ki < ATT_BLOCK, ki >= qi)
    bias_scr[0] = jnp.where(jnp.logical_or(vis_cur, vis_prev), 0.0, NEG_INF)
    bias_scr[1] = jnp.where(vis_cur, 0.0, NEG_INF)

    for gi in reversed(range(N_GROUPS)):
        dilation = DILATION_GROUPS[gi][1]
        per_phase = SPAN // dilation
        nblk = per_phase // ATT_BLOCK
        q_ref = q_refs[gi]
        kc_ref, kp_ref, vc_ref, vp_ref = kv_refs[4 * gi:4 * gi + 4]

        def block(blk, carry, gi=gi, dilation=dilation, per_phase=per_phase, nblk=nblk,
                  q_ref=q_ref, kc_ref=kc_ref, kp_ref=kp_ref, vc_ref=vc_ref, vp_ref=vp_ref):
            r0 = pl.multiple_of(blk * ATT_BLOCK, ATT_BLOCK)
            j = blk % nblk
            phase = blk // nblk
            head_of_phase = j == 0
            q4 = q_ref[pl.ds(r0, ATT_BLOCK), :]
            qs = jnp.concatenate(
                [q4[:, h * HEAD_DIM:(h + 1) * HEAD_DIM] for h in range(Q_PER_KV)], axis=0)
            r_in = pl.multiple_of(jnp.maximum(r0 - ATT_BLOCK, 0), ATT_BLOCK)
            r_out = pl.multiple_of(
                jnp.where(head_of_phase, r0 + per_phase - ATT_BLOCK, 0), ATT_BLOCK)
            k_prev = jnp.where(head_of_phase, kp_ref[pl.ds(r_out, ATT_BLOCK), :],
                               kc_ref[pl.ds(r_in, ATT_BLOCK), :])
            v_prev = jnp.where(head_of_phase, vp_ref[pl.ds(r_out, ATT_BLOCK), :],
                               vc_ref[pl.ds(r_in, ATT_BLOCK), :])
            kcat = jnp.concatenate([k_prev, kc_ref[pl.ds(r0, ATT_BLOCK), :]], axis=0)
            vcat = jnp.concatenate([v_prev, vc_ref[pl.ds(r0, ATT_BLOCK), :]], axis=0)
            s = lax.dot_general(qs, kcat, (((1,), (1,)), ((), ())), preferred_element_type=F32)
            no_prev = jnp.logical_and(head_of_phase, first_span)
            s = s + jnp.where(no_prev, bias_scr[1], bias_scr[0])
            m = jnp.max(s, axis=-1, keepdims=True)
            p = jnp.exp(s - m)
            den = jnp.sum(p, axis=-1, keepdims=True)
            o = _dot(p.astype(BF16), vcat) * (1.0 / den)
            lse = m + jnp.log(den)
            t0 = j * (ATT_BLOCK * dilation) + phase
            for h in range(Q_PER_KV):
                hrows = slice(h * ATT_BLOCK, (h + 1) * ATT_BLOCK)
                oh = o[hrows]
                lh = jnp.broadcast_to(lse[hrows], (ATT_BLOCK, HEAD_DIM))
                if gi > 0:
                    tok = pl.ds(t0, ATT_BLOCK, stride=dilation)
                    o_scr[gi - 1, h, tok, :] = oh
                    l_scr[gi - 1, h, tok, :] = lh
                else:
                    tok = pl.ds(r0, ATT_BLOCK)
                    os_, ls_ = [oh], [lh]
                    for og in range(N_GROUPS - 1):
                        os_.append(o_scr[og, h, tok, :])
                        ls_.append(l_scr[og, h, tok, :])
                    top = functools.reduce(jnp.maximum, ls_)
                    ws = [jnp.exp(l - top) for l in ls_]
                    num = functools.reduce(lambda a, b: a + b, [w * v for w, v in zip(ws, os_)])
                    merged = num / functools.reduce(lambda a, b: a + b, ws)
                    out_ref[tok, h * HEAD_DIM:(h + 1) * HEAD_DIM] = merged.astype(BF16)
            return carry

        lax.fori_loop(0, SPAN // ATT_BLOCK, block, 0, unroll=2)


def _attention(q, kv):
    B, S, _ = q.shape
    qw = Q_PER_KV * HEAD_DIM
    prev = lambda n: jnp.maximum(n - 1, 0)
    in_specs = []
    for gi in range(N_GROUPS):
        in_specs.append(pl.BlockSpec((None, SPAN, qw), lambda b, n, h, gi=gi: (b, n, gi * KV_HEADS + h)))
    for gi in range(N_GROUPS):
        kcol = lambda h, gi=gi: gi * 2 * KV_HEADS + h
        vcol = lambda h, gi=gi: gi * 2 * KV_HEADS + KV_HEADS + h
        for col in (kcol, vcol):
            in_specs.append(pl.BlockSpec((None, SPAN, HEAD_DIM),
                                         lambda b, n, h, col=col: (b, n, col(h))))
            in_specs.append(pl.BlockSpec((None, SPAN, HEAD_DIM),
                                         lambda b, n, h, col=col: (b, prev(n), col(h))))
    return pl.pallas_call(
        _attn_kernel,
        grid=(B, S // SPAN, KV_HEADS),
        in_specs=in_specs,
        out_specs=pl.BlockSpec((None, SPAN, qw), lambda b, n, h: (b, n, h)),
        out_shape=jax.ShapeDtypeStruct((B, S, Q_HEADS * HEAD_DIM), BF16),
        scratch_shapes=[pltpu.VMEM((N_GROUPS - 1, Q_PER_KV, SPAN, HEAD_DIM), F32),
                        pltpu.VMEM((N_GROUPS - 1, Q_PER_KV, SPAN, HEAD_DIM), F32),
                        pltpu.VMEM((2, Q_PER_KV * ATT_BLOCK, 2 * ATT_BLOCK), F32)],
        compiler_params=pltpu.CompilerParams(
            dimension_semantics=("arbitrary",) * 3, vmem_limit_bytes=VMEM_LIMIT),
        name="attn",
    )(*([q] * N_GROUPS), *([kv] * (4 * N_GROUPS)))


def kernel(x, positions, a_norm, a_w_in, a_conv_w, a_conv_b, a_gate_a_w, a_gate_a_b, a_gate_x_w, a_gate_x_b, a_lambda, a_w_out, a_ffn_norm, a_ffn_w_in, a_ffn_w_out, kv_norm, w_kv, k_norm, b_norm, b_w_q, b_q_norm, b_w_o, b_ffn_norm, b_ffn_w_in, b_ffn_w_out):
    assert x.shape[1] % SPAN == 0 and x.shape[2] == D_MODEL
    assert a_norm.shape[0] == 1 and b_norm.shape[0] == 1
    hidden = a_ffn_w_out.shape[1]
    h = x
    w_gates = jnp.concatenate([a_gate_a_w[0], a_gate_x_w[0]], axis=-1).astype(BF16)
    h = _mixer(h, a_norm[0], a_w_in[0].astype(BF16), a_conv_w[0], a_conv_b[0], w_gates,
               a_gate_a_b[0], a_gate_x_b[0], a_lambda[0], a_w_out[0].astype(BF16))
    h = _ffn(h, a_ffn_norm[0], a_ffn_w_in[0][:, :hidden].astype(BF16),
             a_ffn_w_in[0][:, hidden:].astype(BF16), a_ffn_w_out[0].astype(BF16))
    cos_t, sin_t = _rope_tables(positions)
    q, kv = _qkv(h, b_norm[0], kv_norm, b_w_q[0].astype(BF16), w_kv.astype(BF16),
                 b_q_norm[0], k_norm, cos_t, sin_t)
    att = _attention(q, kv)
    h = _ffn(h, b_ffn_norm[0], b_ffn_w_in[0][:, :hidden].astype(BF16),
             b_ffn_w_in[0][:, hidden:].astype(BF16), b_ffn_w_out[0].astype(BF16),
             att=att, w_o=b_w_o[0].astype(BF16))
    return h
```

```python
import functools

import jax
import jax.numpy as jnp
from jax import lax
from jax.experimental import pallas as pl
from jax.experimental.pallas import tpu as pltpu

D_MODEL = 1024
LRU_BLOCKS = 8
LRU_BLOCK_W = D_MODEL // LRU_BLOCKS
CONV_WIDTH = 4
LRU_C = 8.0
HEAD_DIM = 128
Q_HEADS = D_MODEL // HEAD_DIM
KV_HEADS = 2
Q_PER_KV = Q_HEADS // KV_HEADS
DILATION_GROUPS = ((128, 1), (512, 4), (2048, 16))
N_GROUPS = len(DILATION_GROUPS)
ATT_BLOCK = 128
ROPE_DIM = HEAD_DIM // 4
ROPE_HALF = ROPE_DIM // 2
ROPE_THETA = 500000.0
EPS = 1e-6
NEG_INF = -1e30
GELU_C0 = 0.7978845608028654
GELU_C1 = GELU_C0 * 0.044715
SOFTMAX_SHIFT_LIMIT = 40.0

LANES = 128
SUBLANES = 8
N_SLABS = D_MODEL // LANES
SPAN = ATT_BLOCK * max(d for _, d in DILATION_GROUPS)
PROJ_CHUNK = 256
HEAD_PAIR = 2 * HEAD_DIM
assert KV_HEADS * HEAD_DIM == HEAD_PAIR
MIXER_TILE = 512
FFN_TILE = 512
VMEM_LIMIT = 56 * 1024 * 1024

F32 = jnp.float32
BF16 = jnp.bfloat16


def _rms_rows(x):
    return x * lax.rsqrt(jnp.mean(x * x, axis=-1, keepdims=True) + EPS)


def _sigmoid(x):
    return 0.5 * jnp.tanh(0.5 * x) + 0.5


def _silu(x):
    hx = 0.5 * x
    return hx + hx * jnp.tanh(hx)


def _dot(a, b):
    return jnp.dot(a, b, preferred_element_type=F32)


def _mixer_kernel(h_ref, g_ref, win_ref, cw_ref, cb_ref, wg_ref, ba_ref, bx_ref, lam_ref,
                  wout_ref, o_ref, xpad_ref, a_ref, u_ref, hr_ref, carry_ref):
    j = pl.program_id(1)
    tm = h_ref.shape[0]
    x = h_ref[...]
    hn = (_rms_rows(x) * g_ref[...]).astype(BF16)
    proj = _dot(hn, win_ref[...])
    yb = proj[:, :D_MODEL]
    hy = 0.5 * yb
    gate = hy + hy * jnp.tanh(yb * (GELU_C0 + GELU_C1 * (yb * yb)))
    xb = proj[:, D_MODEL:]

    @pl.when(j == 0)
    def _():
        xpad_ref[0:SUBLANES, :] = jnp.zeros((SUBLANES, D_MODEL), F32)
        carry_ref[...] = jnp.zeros_like(carry_ref)

    @pl.when(j > 0)
    def _():
        xpad_ref[0:SUBLANES, :] = xpad_ref[tm:tm + SUBLANES, :]

    xpad_ref[SUBLANES:, :] = xb
    xc = cb_ref[...]
    for k in range(CONV_WIDTH):
        off = SUBLANES - (CONV_WIDTH - 1) + k
        xc = xc + cw_ref[k:k + 1, :] * xpad_ref[off:off + tm, :]

    xcb = xc.astype(BF16)
    ra, rx = [], []
    for i in range(LRU_BLOCKS):
        r = _dot(xcb[:, i * LRU_BLOCK_W:(i + 1) * LRU_BLOCK_W], wg_ref[i])
        ra.append(r[:, :LRU_BLOCK_W])
        rx.append(r[:, LRU_BLOCK_W:])
    r_gate = _sigmoid(jnp.concatenate(ra, axis=1) + ba_ref[...])
    i_gate = _sigmoid(jnp.concatenate(rx, axis=1) + bx_ref[...])
    z = -lam_ref[...]
    softplus = jnp.maximum(z, 0.0) + jnp.log1p(jnp.exp(-jnp.abs(z)))
    log_a = (-LRU_C) * r_gate * softplus
    a = jnp.exp(log_a)
    a_ref[...] = a
    u_ref[...] = jnp.sqrt(-jnp.tanh(log_a) * (1.0 + a * a)) * (i_gate * xc)

    row = lax.broadcasted_iota(jnp.int32, (SUBLANES, D_MODEL), 0)

    def group(gi, hc):
        off = pl.multiple_of(gi * SUBLANES, SUBLANES)
        a = a_ref[pl.ds(off, SUBLANES), :]
        u = u_ref[pl.ds(off, SUBLANES), :]
        for k in (1, 2, 4):
            keep = row >= k
            u = u + jnp.where(keep, a * pltpu.roll(u, k, 0), 0.0)
            a = a * jnp.where(keep, pltpu.roll(a, k, 0), 1.0)
        hg = u + a * hc
        hr_ref[pl.ds(off, SUBLANES), :] = hg
        return jnp.broadcast_to(hg[SUBLANES - 1:SUBLANES, :], (SUBLANES, D_MODEL))

    carry_ref[...] = lax.fori_loop(0, tm // SUBLANES, group, carry_ref[...])

    y = (gate * hr_ref[...]).astype(BF16)
    o_ref[...] = x + _dot(y, wout_ref[...])


def _const_spec(shape):
    nd = len(shape)
    return pl.BlockSpec(shape, lambda *_: (0,) * nd, pipeline_mode=pl.Buffered(1))


def _mixer(h, g, w_in, conv_w, conv_b, w_gates, b_a, b_x, lam, w_out):
    B, S, D = h.shape
    tm = MIXER_TILE
    row = lambda v: v.reshape(1, D)
    tile = pl.BlockSpec((None, tm, D), lambda b, j: (b, j, 0))
    return pl.pallas_call(
        _mixer_kernel,
        grid=(B, S // tm),
        in_specs=[tile, _const_spec((1, D)), _const_spec(w_in.shape), _const_spec(conv_w.shape),
                  _const_spec((1, D)), _const_spec(w_gates.shape), _const_spec((1, D)),
                  _const_spec((1, D)), _const_spec((1, D)), _const_spec(w_out.shape)],
        out_specs=tile,
        out_shape=jax.ShapeDtypeStruct(h.shape, F32),
        scratch_shapes=[pltpu.VMEM((tm + SUBLANES, D), F32), pltpu.VMEM((tm, D), F32),
                        pltpu.VMEM((tm, D), F32), pltpu.VMEM((tm, D), F32),
                        pltpu.VMEM((SUBLANES, D), F32)],
        compiler_params=pltpu.CompilerParams(
            dimension_semantics=("arbitrary", "arbitrary"), vmem_limit_bytes=VMEM_LIMIT),
        name="mixer",
    )(h, row(g), w_in, conv_w, row(conv_b), w_gates, row(b_a), row(b_x), row(lam), w_out)


def _ffn_kernel(*refs, with_proj):
    if with_proj:
        h_ref, att_ref, wo_ref, g_ref, wg_ref, wu_ref, wd_ref, o_ref = refs
        x = h_ref[...] + _dot(att_ref[...], wo_ref[...])
    else:
        h_ref, g_ref, wg_ref, wu_ref, wd_ref, o_ref = refs
        x = h_ref[...]
    hn = (_rms_rows(x) * g_ref[...]).astype(BF16)
    gg = _dot(hn, wg_ref[...])
    uu = _dot(hn, wu_ref[...])
    act = (_silu(gg) * uu).astype(BF16)
    o_ref[...] = x + _dot(act, wd_ref[...])


def _ffn(h, g, w_gate, w_up, w_down, att=None, w_o=None):
    B, S, D = h.shape
    T = B * S
    tm = FFN_TILE
    with_proj = att is not None
    tile = pl.BlockSpec((tm, D), lambda i: (i, 0))
    args = [h.reshape(T, D)]
    specs = [tile]
    if with_proj:
        args += [att.reshape(T, D), w_o]
        specs += [tile, _const_spec(w_o.shape)]
    args += [g.reshape(1, D), w_gate, w_up, w_down]
    specs += [_const_spec((1, D)), _const_spec(w_gate.shape), _const_spec(w_up.shape),
              _const_spec(w_down.shape)]
    out = pl.pallas_call(
        functools.partial(_ffn_kernel, with_proj=with_proj),
        grid=(T // tm,),
        in_specs=specs,
        out_specs=tile,
        out_shape=jax.ShapeDtypeStruct((T, D), F32),
        compiler_params=pltpu.CompilerParams(
            dimension_semantics=("arbitrary",), vmem_limit_bytes=VMEM_LIMIT),
        name="ffn_proj" if with_proj else "ffn",
    )(*args)
    return out.reshape(B, S, D)


def _rope_kernel(pos_ref, invf_ref, sign_ref, cos_ref, sin_ref):
    n = pos_ref.shape[-1]
    posb = jnp.broadcast_to(pos_ref[...], (LANES, n)).T
    ang = posb * invf_ref[...]
    cos_ref[...] = jnp.cos(ang)
    sin_ref[...] = jnp.sin(ang) * sign_ref[...]


def _rope_tables(positions):
    B, S = positions.shape
    inv_freq = ROPE_THETA ** (-jnp.arange(0, ROPE_DIM, 2, dtype=F32) / ROPE_DIM)
    invf = jnp.concatenate([inv_freq, inv_freq, jnp.zeros((LANES - ROPE_DIM,), F32)])[None]
    sign = jnp.concatenate([-jnp.ones((ROPE_HALF,), F32), jnp.ones((LANES - ROPE_HALF,), F32)])[None]
    pos = positions.astype(F32).reshape(B, 1, S)
    table = pl.BlockSpec((None, SPAN, LANES), lambda b, n: (b, n, 0))
    return pl.pallas_call(
        _rope_kernel,
        grid=(B, S // SPAN),
        in_specs=[pl.BlockSpec((None, 1, SPAN), lambda b, n: (b, 0, n)),
                  _const_spec((1, LANES)), _const_spec((1, LANES))],
        out_specs=[table, table],
        out_shape=[jax.ShapeDtypeStruct((B, S, LANES), F32)] * 2,
        compiler_params=pltpu.CompilerParams(dimension_semantics=("arbitrary", "arbitrary")),
        name="rope",
    )(pos, invf, sign)


def _pair_ones():
    shape = (HEAD_PAIR, HEAD_PAIR)
    src = lax.broadcasted_iota(jnp.int32, shape, 0) // HEAD_DIM
    dst = lax.broadcasted_iota(jnp.int32, shape, 1) // HEAD_DIM
    return jnp.where(src == dst, 1.0, 0.0).astype(BF16)


def _norm_rope_pair(t, gain2, cos2, sin2, ones, low_half):
    ss = _dot((t * t).astype(BF16), ones)
    z = t * gain2
    parts = []
    for h in range(HEAD_PAIR // HEAD_DIM):
        zh = z[:, h * HEAD_DIM:(h + 1) * HEAD_DIM]
        parts.append(jnp.where(low_half, pltpu.roll(zh, HEAD_DIM - ROPE_HALF, 1),
                               pltpu.roll(zh, ROPE_HALF, 1)))
    partner = jnp.concatenate(parts, axis=1)
    return lax.rsqrt(ss * (1.0 / HEAD_DIM) + EPS) * (z * cos2 + partner * sin2)


def _qkv_group(slab_refs, gq, gkv, wq_ref, wkv_ref, qn2, kn2, cos_ref, sin_ref, q_out, kv_out,
               dilation):
    per_phase = SPAN // dilation
    n_pieces = max(1, PROJ_CHUNK // per_phase)
    piece = PROJ_CHUNK // n_pieces
    ones = _pair_ones()
    rot = lax.broadcasted_iota(jnp.int32, (PROJ_CHUNK, HEAD_DIM), 1) < ROPE_HALF

    def chunk(c, carry):
        row0 = pl.multiple_of(c * PROJ_CHUNK, PROJ_CHUNK)

        def gather(ref):
            parts = []
            for i in range(n_pieces):
                r = row0 + i * piece
                if dilation == 1:
                    idx = pl.ds(r, piece)
                else:
                    idx = pl.ds((r % per_phase) * dilation + r // per_phase, piece, stride=dilation)
                parts.append(ref[idx, :])
            return parts[0] if n_pieces == 1 else jnp.concatenate(parts, axis=0)

        y = _rms_rows(jnp.concatenate([gather(s) for s in slab_refs], axis=1))
        cosv, sinv = gather(cos_ref), gather(sin_ref)
        cos2 = jnp.concatenate([cosv, cosv], axis=1)
        sin2 = jnp.concatenate([sinv, sinv], axis=1)
        rows = pl.ds(row0, PROJ_CHUNK)
        q = _dot((y * gq).astype(BF16), wq_ref[...])
        for i in range(Q_HEADS * HEAD_DIM // HEAD_PAIR):
            cols = slice(i * HEAD_PAIR, (i + 1) * HEAD_PAIR)
            q_out[rows, cols] = _norm_rope_pair(q[:, cols], qn2, cos2, sin2, ones, rot).astype(BF16)
        kv = _dot((y * gkv).astype(BF16), wkv_ref[...])
        kcols = slice(0, HEAD_PAIR)
        kv_out[rows, kcols] = _norm_rope_pair(kv[:, kcols], kn2, cos2, sin2, ones, rot).astype(BF16)
        vcols = slice(HEAD_PAIR, 2 * HEAD_PAIR)
        kv_out[rows, vcols] = kv[:, vcols].astype(BF16)
        return carry

    lax.fori_loop(0, SPAN // PROJ_CHUNK, chunk, 0, unroll=2)


def _qkv_kernel(*refs):
    slab_refs = refs[:N_SLABS]
    gq_ref, gkv_ref, wq_ref, wkv_ref, qn_ref, kn_ref, cos_ref, sin_ref, q_out, kv_out = refs[N_SLABS:]
    g = pl.program_id(2)
    for gi, (_, dilation) in enumerate(DILATION_GROUPS):
        @pl.when(g == gi)
        def _(gi=gi, dilation=dilation):
            qn = qn_ref[gi:gi + 1, :] * (HEAD_DIM ** -0.5)
            kn = kn_ref[gi:gi + 1, :]
            _qkv_group(slab_refs, gq_ref[...], gkv_ref[...], wq_ref, wkv_ref,
                       jnp.concatenate([qn, qn], axis=1), jnp.concatenate([kn, kn], axis=1),
                       cos_ref, sin_ref, q_out, kv_out, dilation)


def _qkv(h, g_q, g_kv, w_q, w_kv, q_norm, k_norm, cos_t, sin_t):
    B, S, D = h.shape
    qw = Q_HEADS * HEAD_DIM
    kvw = 2 * KV_HEADS * HEAD_DIM
    slab = lambda c: pl.BlockSpec((None, SPAN, LANES), lambda b, n, g: (b, n, c))
    table = pl.BlockSpec((None, SPAN, LANES), lambda b, n, g: (b, n, 0))
    const = lambda shape: pl.BlockSpec(shape, lambda b, n, g: (0,) * len(shape))
    return pl.pallas_call(
        _qkv_kernel,
        grid=(B, S // SPAN, N_GROUPS),
        in_specs=[slab(c) for c in range(N_SLABS)] + [
            const((1, D)), const((1, D)),
            pl.BlockSpec((D, qw), lambda b, n, g: (0, g)),
            pl.BlockSpec((D, kvw), lambda b, n, g: (0, g)),
            const(q_norm.shape), const(k_norm.shape), table, table],
        out_specs=[pl.BlockSpec((None, SPAN, qw), lambda b, n, g: (b, n, g)),
                   pl.BlockSpec((None, SPAN, kvw), lambda b, n, g: (b, n, g))],
        out_shape=[jax.ShapeDtypeStruct((B, S, N_GROUPS * qw), BF16),
                   jax.ShapeDtypeStruct((B, S, N_GROUPS * kvw), BF16)],
        compiler_params=pltpu.CompilerParams(
            dimension_semantics=("arbitrary",) * 3, vmem_limit_bytes=VMEM_LIMIT),
        name="qkv",
    )(*([h] * N_SLABS), g_q.reshape(1, D), g_kv.reshape(1, D), w_q, w_kv, q_norm, k_norm,
      cos_t, sin_t)


def _merge_pitch(dilation):
    if dilation % (2 * SUBLANES) != 0:
        return dilation
    return dilation + SUBLANES


def _attn_span(q_refs, kv_refs, out_ref, acc_scrs, stat_scrs, bias_scr, first_span, shift):
    fast = shift is not None
    rows = Q_PER_KV * ATT_BLOCK

    qi = lax.broadcasted_iota(jnp.int32, (rows, 2 * ATT_BLOCK), 0) % ATT_BLOCK
    ki = lax.broadcasted_iota(jnp.int32, (rows, 2 * ATT_BLOCK), 1)
    vis_cur = jnp.logical_and(ki >= ATT_BLOCK, ki - ATT_BLOCK <= qi)
    vis_prev = jnp.logical_and(ki < ATT_BLOCK, ki >= qi)
    base = -shift if fast else 0.0
    bias_scr[0] = jnp.where(jnp.logical_or(vis_cur, vis_prev), base, NEG_INF)
    bias_scr[1] = jnp.where(vis_cur, base, NEG_INF)
    ones_cols = jnp.ones((2 * ATT_BLOCK, HEAD_DIM), BF16)

    for gi in reversed(range(N_GROUPS)):
        dilation = DILATION_GROUPS[gi][1]
        per_phase = SPAN // dilation
        nblk = per_phase // ATT_BLOCK
        pitch = _merge_pitch(dilation)
        q_ref = q_refs[gi]
        kc_ref, kp_ref, vc_ref, vp_ref = kv_refs[4 * gi:4 * gi + 4]

        def block(blk, carry, gi=gi, per_phase=per_phase, nblk=nblk, pitch=pitch,
                  q_ref=q_ref, kc_ref=kc_ref, kp_ref=kp_ref, vc_ref=vc_ref, vp_ref=vp_ref):
            r0 = pl.multiple_of(blk * ATT_BLOCK, ATT_BLOCK)
            j = blk % nblk
            phase = blk // nblk
            head_of_phase = j == 0
            q4 = q_ref[pl.ds(r0, ATT_BLOCK), :]
            qs = jnp.concatenate(
                [q4[:, h * HEAD_DIM:(h + 1) * HEAD_DIM] for h in range(Q_PER_KV)], axis=0)
            r_in = pl.multiple_of(jnp.maximum(r0 - ATT_BLOCK, 0), ATT_BLOCK)
            r_out = pl.multiple_of(
                jnp.where(head_of_phase, r0 + per_phase - ATT_BLOCK, 0), ATT_BLOCK)
            k_prev = jnp.where(head_of_phase, kp_ref[pl.ds(r_out, ATT_BLOCK), :],
                               kc_ref[pl.ds(r_in, ATT_BLOCK), :])
            v_prev = jnp.where(head_of_phase, vp_ref[pl.ds(r_out, ATT_BLOCK), :],
                               vc_ref[pl.ds(r_in, ATT_BLOCK), :])
            kcat = jnp.concatenate([k_prev, kc_ref[pl.ds(r0, ATT_BLOCK), :]], axis=0)
            vcat = jnp.concatenate([v_prev, vc_ref[pl.ds(r0, ATT_BLOCK), :]], axis=0)
            s = lax.dot_general(qs, kcat, (((1,), (1,)), ((), ())), preferred_element_type=F32)
            no_prev = jnp.logical_and(head_of_phase, first_span)
            s = s + jnp.where(no_prev, bias_scr[1], bias_scr[0])
            if fast:
                av = _dot(jnp.exp(s).astype(BF16), jnp.concatenate([vcat, ones_cols], axis=1))
                acc, stat = av[:, :HEAD_DIM], av[:, HEAD_DIM:]
            else:
                m = jnp.max(s, axis=-1, keepdims=True)
                p = jnp.exp(s - m)
                den = jnp.sum(p, axis=-1, keepdims=True)
                acc = _dot(p.astype(BF16), vcat) * (1.0 / den)
                stat = jnp.broadcast_to(m + jnp.log(den), (rows, HEAD_DIM))
            for h in range(Q_PER_KV):
                hrows = slice(h * ATT_BLOCK, (h + 1) * ATT_BLOCK)
                if gi > 0:
                    tok = pl.ds(j * (ATT_BLOCK * pitch) + phase, ATT_BLOCK, stride=pitch)
                    acc_scrs[gi - 1][h, tok, :] = acc[hrows]
                    stat_scrs[gi - 1][h, tok, :] = stat[hrows]
                    continue
                accs, stats = [acc[hrows]], [stat[hrows]]
                for og in range(1, N_GROUPS):
                    od = DILATION_GROUPS[og][1]
                    op = _merge_pitch(od)
                    n_parts, n = (1, ATT_BLOCK) if op == od else (ATT_BLOCK // od, od)
                    a_parts, s_parts = [], []
                    for i in range(n_parts):
                        src = pl.ds(pl.multiple_of((r0 // od + i) * op, SUBLANES), n)
                        a_parts.append(acc_scrs[og - 1][h, src, :])
                        s_parts.append(stat_scrs[og - 1][h, src, :])
                    accs.append(jnp.concatenate(a_parts, axis=0))
                    stats.append(jnp.concatenate(s_parts, axis=0))
                if fast:
                    merged = (accs[0] + accs[1] + accs[2]) / (stats[0] + stats[1] + stats[2])
                else:
                    top = functools.reduce(jnp.maximum, stats)
                    ws = [jnp.exp(l - top) for l in stats]
                    num = ws[0] * accs[0] + ws[1] * accs[1] + ws[2] * accs[2]
                    merged = num / (ws[0] + ws[1] + ws[2])
                out_ref[pl.ds(r0, ATT_BLOCK), h * HEAD_DIM:(h + 1) * HEAD_DIM] = merged.astype(BF16)
            return carry

        lax.fori_loop(0, SPAN // ATT_BLOCK, block, 0, unroll=8 if fast else 2)


def _attn_kernel(bound_ref, *refs):
    q_refs = refs[0:N_GROUPS]
    kv_refs = refs[N_GROUPS:5 * N_GROUPS]
    out_ref = refs[5 * N_GROUPS]
    scr = refs[5 * N_GROUPS + 1:]
    acc_scrs, stat_scrs, bias_scr = scr[:N_GROUPS - 1], scr[N_GROUPS - 1:2 * (N_GROUPS - 1)], scr[-1]
    first_span = pl.program_id(1) == 0
    bound = bound_ref[0]
    small = bound < SOFTMAX_SHIFT_LIMIT

    @pl.when(small)
    def _():
        _attn_span(q_refs, kv_refs, out_ref, acc_scrs, stat_scrs, bias_scr, first_span, bound)

    @pl.when(jnp.logical_not(small))
    def _():
        _attn_span(q_refs, kv_refs, out_ref, acc_scrs, stat_scrs, bias_scr, first_span, None)


def _attention(q, kv, bound):
    B, S, _ = q.shape
    qw = Q_PER_KV * HEAD_DIM
    prev = lambda n: jnp.maximum(n - 1, 0)
    in_specs = [pl.BlockSpec(memory_space=pltpu.SMEM)]
    for gi in range(N_GROUPS):
        in_specs.append(pl.BlockSpec((None, SPAN, qw), lambda b, n, h, gi=gi: (b, n, gi * KV_HEADS + h)))
    for gi in range(N_GROUPS):
        kcol = lambda h, gi=gi: gi * 2 * KV_HEADS + h
        vcol = lambda h, gi=gi: gi * 2 * KV_HEADS + KV_HEADS + h
        for col in (kcol, vcol):
            in_specs.append(pl.BlockSpec((None, SPAN, HEAD_DIM),
                                         lambda b, n, h, col=col: (b, n, col(h))))
            in_specs.append(pl.BlockSpec((None, SPAN, HEAD_DIM),
                                         lambda b, n, h, col=col: (b, prev(n), col(h))))
    merge_scratch = lambda: [
        pltpu.VMEM((Q_PER_KV, SPAN // d * _merge_pitch(d), HEAD_DIM), F32)
        for _, d in DILATION_GROUPS[1:]]
    return pl.pallas_call(
        _attn_kernel,
        grid=(B, S // SPAN, KV_HEADS),
        in_specs=in_specs,
        out_specs=pl.BlockSpec((None, SPAN, qw), lambda b, n, h: (b, n, h)),
        out_shape=jax.ShapeDtypeStruct((B, S, Q_HEADS * HEAD_DIM), BF16),
        scratch_shapes=merge_scratch() + merge_scratch() + [
            pltpu.VMEM((2, Q_PER_KV * ATT_BLOCK, 2 * ATT_BLOCK), F32)],
        compiler_params=pltpu.CompilerParams(
            dimension_semantics=("arbitrary",) * 3, vmem_limit_bytes=VMEM_LIMIT),
        name="attn",
    )(bound, *([q] * N_GROUPS), *([kv] * (4 * N_GROUPS)))


def kernel(x, positions, a_norm, a_w_in, a_conv_w, a_conv_b, a_gate_a_w, a_gate_a_b, a_gate_x_w, a_gate_x_b, a_lambda, a_w_out, a_ffn_norm, a_ffn_w_in, a_ffn_w_out, kv_norm, w_kv, k_norm, b_norm, b_w_q, b_q_norm, b_w_o, b_ffn_norm, b_ffn_w_in, b_ffn_w_out):
    assert x.shape[1] % SPAN == 0 and x.shape[2] == D_MODEL
    assert a_norm.shape[0] == 1 and b_norm.shape[0] == 1
    hidden = a_ffn_w_out.shape[1]
    h = x
    w_gates = jnp.concatenate([a_gate_a_w[0], a_gate_x_w[0]], axis=-1).astype(BF16)
    h = _mixer(h, a_norm[0], a_w_in[0].astype(BF16), a_conv_w[0], a_conv_b[0], w_gates,
               a_gate_a_b[0], a_gate_x_b[0], a_lambda[0], a_w_out[0].astype(BF16))
    h = _ffn(h, a_ffn_norm[0], a_ffn_w_in[0][:, :hidden].astype(BF16),
             a_ffn_w_in[0][:, hidden:].astype(BF16), a_ffn_w_out[0].astype(BF16))
    cos_t, sin_t = _rope_tables(positions)
    q, kv = _qkv(h, b_norm[0], kv_norm, b_w_q[0].astype(BF16), w_kv.astype(BF16),
                 b_q_norm[0], k_norm, cos_t, sin_t)
    gain_prod = jnp.max(jnp.abs(b_q_norm[0]), axis=-1) * jnp.max(jnp.abs(k_norm), axis=-1)
    bound = (1.02 * HEAD_DIM ** 0.5) * jnp.max(gain_prod).reshape(1)
    att = _attention(q, kv, bound)
    h = _ffn(h, b_ffn_norm[0], b_ffn_w_in[0][:, :hidden].astype(BF16),
             b_ffn_w_in[0][:, hidden:].astype(BF16), b_ffn_w_out[0].astype(BF16),
             att=att, w_o=b_w_o[0].astype(BF16))
    return h
```

```python
import functools

import jax
import jax.numpy as jnp
from jax import lax
from jax.experimental import pallas as pl
from jax.experimental.pallas import tpu as pltpu

D_MODEL = 1024
LRU_BLOCKS = 8
LRU_BLOCK_W = D_MODEL // LRU_BLOCKS
CONV_WIDTH = 4
LRU_C = 8.0
HEAD_DIM = 128
Q_HEADS = D_MODEL // HEAD_DIM
KV_HEADS = 2
Q_PER_KV = Q_HEADS // KV_HEADS
DILATION_GROUPS = ((128, 1), (512, 4), (2048, 16))
N_GROUPS = len(DILATION_GROUPS)
ATT_BLOCK = 128
ROPE_DIM = HEAD_DIM // 4
ROPE_HALF = ROPE_DIM // 2
ROPE_PACK = 128 // ROPE_DIM
ROPE_THETA = 500000.0
EPS = 1e-6
NEG_INF = -1e30
GELU_C0 = 0.7978845608028654
GELU_C1 = GELU_C0 * 0.044715
F32_TINY = 1.1754943508222875e-38
SOFTMAX_SHIFT_LIMIT = 40.0

LANES = 128
SUBLANES = 8
N_SLABS = D_MODEL // LANES
SPAN = ATT_BLOCK * max(d for _, d in DILATION_GROUPS)
PROJ_CHUNK = 256
MXU_TILE = 256
HEAD_PAIR = 2 * HEAD_DIM
assert HEAD_PAIR == MXU_TILE and KV_HEADS * HEAD_DIM == HEAD_PAIR
LAYER_A_TILE = 256
FFN_TILE = 512
VMEM_LIMIT = 56 * 1024 * 1024

F32 = jnp.float32
BF16 = jnp.bfloat16


def _rms_rows(x):
    return x * lax.rsqrt(jnp.mean(x * x, axis=-1, keepdims=True) + EPS)


def _sigmoid_of_half(hx):
    return 0.5 * jnp.tanh(hx) + 0.5


def _silu_of_half(hx):
    return hx + hx * jnp.tanh(hx)


def _gelu_of_half(hx):
    return hx + hx * jnp.tanh(hx * (2.0 * GELU_C0 + 8.0 * GELU_C1 * (hx * hx)))


def _dot(a, b):
    return jnp.dot(a, b, preferred_element_type=F32)


def _const_spec(shape):
    nd = len(shape)
    return pl.BlockSpec(shape, lambda *_: (0,) * nd, pipeline_mode=pl.Buffered(1))


def _swiglu_residual(x, g_ref, wg_ref, wu_ref, wd_ref):
    hn = (_rms_rows(x) * g_ref[...]).astype(BF16)
    half_g = _dot(hn, wg_ref[...])
    uu = _dot(hn, wu_ref[...])
    return x + _dot((_silu_of_half(half_g) * uu).astype(BF16), wd_ref[...])


def _mixer_chunk(half_y, xb, c, cw_ref, cb_ref, wgates_ref, ba_ref, bx_ref, decay, xpad_ref,
                 carry_ref, y_ref):
    tm = xb.shape[0]
    cols = slice(c * MXU_TILE, (c + 1) * MXU_TILE)
    gate = _gelu_of_half(half_y)
    xpad_ref[SUBLANES:, cols] = xb
    xc = cb_ref[:, cols]
    for k in range(CONV_WIDTH):
        off = SUBLANES - (CONV_WIDTH - 1) + k
        xc = xc + cw_ref[k:k + 1, cols] * xpad_ref[off:off + tm, cols]

    xcb = xc.astype(BF16)
    ra, rx = [], []
    for b in range(MXU_TILE // LRU_BLOCK_W):
        r = _dot(xcb[:, b * LRU_BLOCK_W:(b + 1) * LRU_BLOCK_W],
                 wgates_ref[c * (MXU_TILE // LRU_BLOCK_W) + b])
        ra.append(r[:, :LRU_BLOCK_W])
        rx.append(r[:, LRU_BLOCK_W:])
    r_gate = _sigmoid_of_half(jnp.concatenate(ra, axis=1) + ba_ref[:, cols])
    i_gate = _sigmoid_of_half(jnp.concatenate(rx, axis=1) + bx_ref[:, cols])
    log_a = decay[:, cols] * r_gate
    a_all = jnp.exp(log_a)
    one_minus_a2 = -jnp.tanh(log_a) * (1.0 + a_all * a_all)
    u_all = (one_minus_a2 * lax.rsqrt(jnp.maximum(one_minus_a2, F32_TINY))) * (i_gate * xc)

    row = lax.broadcasted_iota(jnp.int32, (SUBLANES, MXU_TILE), 0)
    hc = carry_ref[:, cols]
    groups = []
    for gi in range(tm // SUBLANES):
        rows = slice(gi * SUBLANES, (gi + 1) * SUBLANES)
        a, u = a_all[rows], u_all[rows]
        for k in (1, 2, 4):
            keep = row >= k
            u = u + jnp.where(keep, a * pltpu.roll(u, k, 0), 0.0)
            a = a * jnp.where(keep, pltpu.roll(a, k, 0), 1.0)
        hg = u + a * hc
        groups.append(hg)
        hc = jnp.broadcast_to(hg[SUBLANES - 1:SUBLANES, :], (SUBLANES, MXU_TILE))
    carry_ref[:, cols] = hc
    y_ref[:, cols] = (gate * jnp.concatenate(groups, axis=0)).astype(BF16)


def _layer_a_kernel(hcur_ref, hprev_ref, g_ref, win_ref, cw_ref, cb_ref, wgates_ref, ba_ref, bx_ref,
                    lam_ref, wout_ref, gf_ref, wg_ref, wu_ref, wd_ref, o_ref,
                    y_ref, xpad_ref, carry_ref, h1_ref, act_ref, *, tiles_per_seq):
    i = pl.program_id(0)
    j = i % tiles_per_seq
    tm = hcur_ref.shape[0]

    @pl.when(i == 0)
    def _():
        y_ref[...] = jnp.zeros_like(y_ref)
        h1_ref[...] = jnp.zeros_like(h1_ref)
        act_ref[...] = jnp.zeros_like(act_ref)

    @pl.when(j == 0)
    def _():
        xpad_ref[0:SUBLANES, :] = jnp.zeros((SUBLANES, D_MODEL), F32)
        carry_ref[...] = jnp.zeros_like(carry_ref)

    @pl.when(j > 0)
    def _():
        xpad_ref[0:SUBLANES, :] = xpad_ref[tm:tm + SUBLANES, :]

    new = i % 2
    old = 1 - new
    h1 = hprev_ref[...] + _dot(y_ref[old], wout_ref[...])
    hn_ffn = (_rms_rows(h1) * gf_ref[...]).astype(BF16)
    hn_mix = (_rms_rows(hcur_ref[...]) * g_ref[...]).astype(BF16)
    proj = _dot(hn_mix, win_ref[...])
    z = -lam_ref[...]
    decay = (-LRU_C) * (jnp.maximum(z, 0.0) + jnp.log1p(jnp.exp(-jnp.abs(z))))

    n_mix = D_MODEL // MXU_TILE
    n_ffn = wd_ref.shape[0] // MXU_TILE
    gg, uu = [], []
    for c in range(n_mix):
        hid = slice(c * n_ffn // n_mix * MXU_TILE, (c + 1) * n_ffn // n_mix * MXU_TILE)
        gg.append(_dot(hn_ffn, wg_ref[:, hid]))
        uu.append(_dot(hn_ffn, wu_ref[:, hid]))
        cols = slice(c * MXU_TILE, (c + 1) * MXU_TILE)
        _mixer_chunk(proj[:, cols], proj[:, D_MODEL + c * MXU_TILE:D_MODEL + (c + 1) * MXU_TILE], c,
                     cw_ref, cb_ref, wgates_ref, ba_ref, bx_ref, decay, xpad_ref, carry_ref,
                     y_ref.at[new])
        o_ref[:, cols] = h1_ref[old, :, cols] + _dot(act_ref[old], wd_ref[:, cols])
    h1_ref[new] = h1
    act_ref[new] = (_silu_of_half(jnp.concatenate(gg, axis=1))
                    * jnp.concatenate(uu, axis=1)).astype(BF16)


def _layer_a(h, g, w_in, conv_w, conv_b, w_gates, b_a, b_x, lam, w_out, g_ffn, w_gate, w_up,
             w_down):
    B, S, D = h.shape
    T = B * S
    tm = LAYER_A_TILE
    n_tiles = T // tm
    row = lambda v: v.reshape(1, D)
    lag = lambda k: pl.BlockSpec((tm, D), lambda i: (jnp.clip(i - k, 0, n_tiles - 1), 0))
    h2 = h.reshape(T, D)
    consts = [row(g), w_in, conv_w, row(conv_b), w_gates, row(b_a), row(b_x), row(lam), w_out,
              row(g_ffn), w_gate, w_up, w_down]
    out = pl.pallas_call(
        functools.partial(_layer_a_kernel, tiles_per_seq=S // tm),
        grid=(n_tiles + 2,),
        in_specs=[lag(0), lag(1)] + [_const_spec(c.shape) for c in consts],
        out_specs=lag(2),
        out_shape=jax.ShapeDtypeStruct((T, D), F32),
        scratch_shapes=[pltpu.VMEM((2, tm, D), BF16), pltpu.VMEM((tm + SUBLANES, D), F32),
                        pltpu.VMEM((SUBLANES, D), F32), pltpu.VMEM((2, tm, D), F32),
                        pltpu.VMEM((2, tm, w_down.shape[0]), BF16)],
        compiler_params=pltpu.CompilerParams(
            dimension_semantics=("arbitrary",), vmem_limit_bytes=VMEM_LIMIT),
        name="layer_a",
    )(h2, h2, *consts)
    return out.reshape(B, S, D)


def _layer_b_out_kernel(h_ref, att_ref, wo_ref, g_ref, wg_ref, wu_ref, wd_ref, o_ref):
    x = h_ref[...] + _dot(att_ref[...], wo_ref[...])
    o_ref[...] = _swiglu_residual(x, g_ref, wg_ref, wu_ref, wd_ref)


def _layer_b_out(h, att, w_o, g, w_gate, w_up, w_down):
    B, S, D = h.shape
    T = B * S
    tm = FFN_TILE
    tile = pl.BlockSpec((tm, D), lambda i: (i, 0))
    consts = [w_o, g.reshape(1, D), w_gate, w_up, w_down]
    out = pl.pallas_call(
        _layer_b_out_kernel,
        grid=(T // tm,),
        in_specs=[tile, tile] + [_const_spec(c.shape) for c in consts],
        out_specs=tile,
        out_shape=jax.ShapeDtypeStruct((T, D), F32),
        compiler_params=pltpu.CompilerParams(
            dimension_semantics=("arbitrary",), vmem_limit_bytes=VMEM_LIMIT),
        name="ffn_proj",
    )(h.reshape(T, D), att.reshape(T, D), *consts)
    return out.reshape(B, S, D)


def _rope_kernel(pos_ref, invf_ref, sign_ref, cos_ref, sin_ref):
    n = pos_ref.shape[-1]
    q = n // ROPE_PACK
    posb = jnp.broadcast_to(pos_ref[...], (LANES, n)).T
    lane = lax.broadcasted_iota(jnp.int32, (q, LANES), 1)
    packed = posb[0:q]
    for k in range(1, ROPE_PACK):
        packed = jnp.where(lane >= k * ROPE_DIM, posb[k * q:(k + 1) * q], packed)
    ang = packed * invf_ref[...]
    cos_p = jnp.cos(ang)
    sin_p = jnp.sin(ang) * sign_ref[...]
    for k in range(ROPE_PACK):
        shift = (LANES - k * ROPE_DIM) % LANES
        cos_k = pltpu.roll(cos_p, shift, 1) if shift else cos_p
        sin_k = pltpu.roll(sin_p, shift, 1) if shift else sin_p
        cos_ref[k * q:(k + 1) * q, :] = jnp.where(lane < ROPE_DIM, cos_k, 1.0)
        sin_ref[k * q:(k + 1) * q, :] = jnp.where(lane < ROPE_DIM, sin_k, 0.0)


def _rope_tables(positions):
    B, S = positions.shape
    inv_freq = ROPE_THETA ** (-jnp.arange(0, ROPE_DIM, 2, dtype=F32) / ROPE_DIM)
    invf = jnp.tile(jnp.concatenate([inv_freq, inv_freq]), ROPE_PACK)[None]
    sign = jnp.tile(jnp.concatenate([-jnp.ones((ROPE_HALF,), F32), jnp.ones((ROPE_HALF,), F32)]),
                    ROPE_PACK)[None]
    pos = positions.astype(F32).reshape(B, 1, S)
    table = pl.BlockSpec((None, SPAN, LANES), lambda b, n: (b, n, 0))
    return pl.pallas_call(
        _rope_kernel,
        grid=(B, S // SPAN),
        in_specs=[pl.BlockSpec((None, 1, SPAN), lambda b, n: (b, 0, n)),
                  _const_spec((1, LANES)), _const_spec((1, LANES))],
        out_specs=[table, table],
        out_shape=[jax.ShapeDtypeStruct((B, S, LANES), F32)] * 2,
        compiler_params=pltpu.CompilerParams(dimension_semantics=("arbitrary", "arbitrary")),
        name="rope",
    )(pos, invf, sign)


def _pair_ones():
    shape = (HEAD_PAIR, HEAD_PAIR)
    src = lax.broadcasted_iota(jnp.int32, shape, 0) // HEAD_DIM
    dst = lax.broadcasted_iota(jnp.int32, shape, 1) // HEAD_DIM
    return jnp.where(src == dst, 1.0, 0.0).astype(BF16)


def _norm_rope_pair(t, gain2, cos2, sin2, ones, low_half):
    ss = _dot((t * t).astype(BF16), ones)
    z = t * gain2
    parts = []
    for h in range(HEAD_PAIR // HEAD_DIM):
        zh = z[:, h * HEAD_DIM:(h + 1) * HEAD_DIM]
        parts.append(jnp.where(low_half, pltpu.roll(zh, HEAD_DIM - ROPE_HALF, 1),
                               pltpu.roll(zh, ROPE_HALF, 1)))
    partner = jnp.concatenate(parts, axis=1)
    return lax.rsqrt(ss * (1.0 / HEAD_DIM) + EPS) * (z * cos2 + partner * sin2)


def _qkv_group(src_refs, src_dilation, stage_refs, gq, gkv, wq_ref, wkv_ref, qn2, kn2, q_out,
               kv_out, dilation):
    per_phase = SPAN // dilation
    n_pieces = max(1, PROJ_CHUNK // per_phase)
    piece = PROJ_CHUNK // n_pieces
    step = dilation // src_dilation
    ones = _pair_ones()
    rot = lax.broadcasted_iota(jnp.int32, (PROJ_CHUNK, HEAD_DIM), 1) < ROPE_HALF

    def chunk(c, carry):
        row0 = pl.multiple_of(c * PROJ_CHUNK, PROJ_CHUNK)

        def gather(ref):
            parts = []
            for i in range(n_pieces):
                r = row0 + i * piece
                phase, l0 = r // per_phase, r % per_phase
                start = (phase % src_dilation) * (SPAN // src_dilation) + l0 * step + phase // src_dilation
                idx = pl.ds(start, piece, stride=step) if step > 1 else pl.ds(start, piece)
                parts.append(ref[idx, :])
            return parts[0] if n_pieces == 1 else jnp.concatenate(parts, axis=0)

        gathered = [gather(s) for s in src_refs]
        if stage_refs is not None:
            for dst, val in zip(stage_refs, gathered):
                dst[pl.ds(row0, PROJ_CHUNK), :] = val
        y = _rms_rows(jnp.concatenate(gathered[:N_SLABS], axis=1))
        cosv, sinv = gathered[N_SLABS:]
        cos2 = jnp.concatenate([cosv, cosv], axis=1)
        sin2 = jnp.concatenate([sinv, sinv], axis=1)
        rows = pl.ds(row0, PROJ_CHUNK)
        q = _dot((y * gq).astype(BF16), wq_ref[...])
        for i in range(Q_HEADS * HEAD_DIM // HEAD_PAIR):
            cols = slice(i * HEAD_PAIR, (i + 1) * HEAD_PAIR)
            q_out[rows, cols] = _norm_rope_pair(q[:, cols], qn2, cos2, sin2, ones, rot).astype(BF16)
        kv = _dot((y * gkv).astype(BF16), wkv_ref[...])
        kcols = slice(0, HEAD_PAIR)
        kv_out[rows, kcols] = _norm_rope_pair(kv[:, kcols], kn2, cos2, sin2, ones, rot).astype(BF16)
        vcols = slice(HEAD_PAIR, 2 * HEAD_PAIR)
        kv_out[rows, vcols] = kv[:, vcols].astype(BF16)
        return carry

    lax.fori_loop(0, SPAN // PROJ_CHUNK, chunk, 0, unroll=2)


def _staging_plan():
    dils = [d for _, d in DILATION_GROUPS]
    src = [None] * len(dils)
    for gi in range(1, len(dils)):
        if dils[gi - 1] > 1 and dils[gi] % dils[gi - 1] == 0 and _merge_pitch(dils[gi]) != dils[gi]:
            src[gi] = gi - 1
    return [(src[gi], gi in src) for gi in range(len(dils))]


def _qkv_kernel(*refs):
    assert sum(staged for _, staged in _staging_plan()) <= 1
    slab_refs = refs[:N_SLABS]
    gq_ref, gkv_ref, wq_ref, wkv_ref, qn_ref, kn_ref, cos_ref, sin_ref, q_out, kv_out = (
        refs[N_SLABS:N_SLABS + 10])
    stage_refs = refs[N_SLABS + 10:]
    token_order = list(slab_refs) + [cos_ref, sin_ref]
    g = pl.program_id(2)
    for gi, ((_, dilation), (src, staged)) in enumerate(zip(DILATION_GROUPS, _staging_plan())):
        @pl.when(g == gi)
        def _(gi=gi, dilation=dilation, src=src, staged=staged):
            qn = qn_ref[gi:gi + 1, :] * (HEAD_DIM ** -0.5)
            kn = kn_ref[gi:gi + 1, :]
            _qkv_group(token_order if src is None else stage_refs,
                       1 if src is None else DILATION_GROUPS[src][1],
                       stage_refs if staged else None, gq_ref[...], gkv_ref[...], wq_ref, wkv_ref,
                       jnp.concatenate([qn, qn], axis=1), jnp.concatenate([kn, kn], axis=1),
                       q_out, kv_out, dilation)


def _qkv(h, g_q, g_kv, w_q, w_kv, q_norm, k_norm, cos_t, sin_t):
    B, S, D = h.shape
    qw = Q_HEADS * HEAD_DIM
    kvw = 2 * KV_HEADS * HEAD_DIM
    slab = lambda c: pl.BlockSpec((None, SPAN, LANES), lambda b, n, g: (b, n, c))
    table = pl.BlockSpec((None, SPAN, LANES), lambda b, n, g: (b, n, 0))
    const = lambda shape: pl.BlockSpec(shape, lambda b, n, g: (0,) * len(shape))
    return pl.pallas_call(
        _qkv_kernel,
        grid=(B, S // SPAN, N_GROUPS),
        in_specs=[slab(c) for c in range(N_SLABS)] + [
            const((1, D)), const((1, D)),
            pl.BlockSpec((D, qw), lambda b, n, g: (0, g)),
            pl.BlockSpec((D, kvw), lambda b, n, g: (0, g)),
            const(q_norm.shape), const(k_norm.shape), table, table],
        out_specs=[pl.BlockSpec((None, SPAN, qw), lambda b, n, g: (b, n, g)),
                   pl.BlockSpec((None, SPAN, kvw), lambda b, n, g: (b, n, g))],
        out_shape=[jax.ShapeDtypeStruct((B, S, N_GROUPS * qw), BF16),
                   jax.ShapeDtypeStruct((B, S, N_GROUPS * kvw), BF16)],
        scratch_shapes=[pltpu.VMEM((SPAN, LANES), F32) for _ in range(N_SLABS + 2)],
        compiler_params=pltpu.CompilerParams(
            dimension_semantics=("arbitrary",) * 3, vmem_limit_bytes=VMEM_LIMIT),
        name="qkv",
    )(*([h] * N_SLABS), g_q.reshape(1, D), g_kv.reshape(1, D), w_q, w_kv, q_norm, k_norm,
      cos_t, sin_t)


def _merge_pitch(dilation):
    if dilation % (2 * SUBLANES) != 0:
        return dilation
    return dilation + SUBLANES


def _attn_span(q_refs, kv_refs, out_ref, acc_scrs, stat_scrs, bias_scr, first_span, shift):
    fast = shift is not None
    rows = Q_PER_KV * ATT_BLOCK

    qi = lax.broadcasted_iota(jnp.int32, (rows, 2 * ATT_BLOCK), 0) % ATT_BLOCK
    ki = lax.broadcasted_iota(jnp.int32, (rows, 2 * ATT_BLOCK), 1)
    vis_cur = jnp.logical_and(ki >= ATT_BLOCK, ki - ATT_BLOCK <= qi)
    vis_prev = jnp.logical_and(ki < ATT_BLOCK, ki >= qi)
    base = -shift if fast else 0.0
    bias_scr[0] = jnp.where(jnp.logical_or(vis_cur, vis_prev), base, NEG_INF)
    bias_scr[1] = jnp.where(vis_cur, base, NEG_INF)
    ones_cols = jnp.ones((2 * ATT_BLOCK, HEAD_DIM), BF16)

    for gi in reversed(range(N_GROUPS)):
        dilation = DILATION_GROUPS[gi][1]
        per_phase = SPAN // dilation
        nblk = per_phase // ATT_BLOCK
        pitch = _merge_pitch(dilation)
        q_ref = q_refs[gi]
        kc_ref, kp_ref, vc_ref, vp_ref = kv_refs[4 * gi:4 * gi + 4]

        def block(blk, carry, gi=gi, per_phase=per_phase, nblk=nblk, pitch=pitch,
                  q_ref=q_ref, kc_ref=kc_ref, kp_ref=kp_ref, vc_ref=vc_ref, vp_ref=vp_ref):
            r0 = pl.multiple_of(blk * ATT_BLOCK, ATT_BLOCK)
            j = blk % nblk
            phase = blk // nblk
            head_of_phase = j == 0
            q4 = q_ref[pl.ds(r0, ATT_BLOCK), :]
            qs = jnp.concatenate(
                [q4[:, h * HEAD_DIM:(h + 1) * HEAD_DIM] for h in range(Q_PER_KV)], axis=0)
            r_in = pl.multiple_of(jnp.maximum(r0 - ATT_BLOCK, 0), ATT_BLOCK)
            r_out = pl.multiple_of(
                jnp.where(head_of_phase, r0 + per_phase - ATT_BLOCK, 0), ATT_BLOCK)
            k_prev = jnp.where(head_of_phase, kp_ref[pl.ds(r_out, ATT_BLOCK), :],
                               kc_ref[pl.ds(r_in, ATT_BLOCK), :])
            v_prev = jnp.where(head_of_phase, vp_ref[pl.ds(r_out, ATT_BLOCK), :],
                               vc_ref[pl.ds(r_in, ATT_BLOCK), :])
            kcat = jnp.concatenate([k_prev, kc_ref[pl.ds(r0, ATT_BLOCK), :]], axis=0)
            vcat = jnp.concatenate([v_prev, vc_ref[pl.ds(r0, ATT_BLOCK), :]], axis=0)
            s = lax.dot_general(qs, kcat, (((1,), (1,)), ((), ())), preferred_element_type=F32)
            no_prev = jnp.logical_and(head_of_phase, first_span)
            s = s + jnp.where(no_prev, bias_scr[1], bias_scr[0])
            if fast:
                av = _dot(jnp.exp(s).astype(BF16), jnp.concatenate([vcat, ones_cols], axis=1))
                acc, stat = av[:, :HEAD_DIM], av[:, HEAD_DIM:]
            else:
                m = jnp.max(s, axis=-1, keepdims=True)
                p = jnp.exp(s - m)
                den = jnp.sum(p, axis=-1, keepdims=True)
                acc = _dot(p.astype(BF16), vcat) * (1.0 / den)
                stat = jnp.broadcast_to(m + jnp.log(den), (rows, HEAD_DIM))
            for h in range(Q_PER_KV):
                hrows = slice(h * ATT_BLOCK, (h + 1) * ATT_BLOCK)
                if gi > 0:
                    tok = pl.ds(j * (ATT_BLOCK * pitch) + phase, ATT_BLOCK, stride=pitch)
                    acc_scrs[gi - 1][h, tok, :] = acc[hrows]
                    stat_scrs[gi - 1][h, tok, :] = stat[hrows]
                    continue
                accs, stats = [acc[hrows]], [stat[hrows]]
                for og in range(1, N_GROUPS):
                    od = DILATION_GROUPS[og][1]
                    op = _merge_pitch(od)
                    n_parts, n = (1, ATT_BLOCK) if op == od else (ATT_BLOCK // od, od)
                    a_parts, s_parts = [], []
                    for i in range(n_parts):
                        src = pl.ds(pl.multiple_of((r0 // od + i) * op, SUBLANES), n)
                        a_parts.append(acc_scrs[og - 1][h, src, :])
                        s_parts.append(stat_scrs[og - 1][h, src, :])
                    accs.append(jnp.concatenate(a_parts, axis=0))
                    stats.append(jnp.concatenate(s_parts, axis=0))
                if fast:
                    merged = (accs[0] + accs[1] + accs[2]) / (stats[0] + stats[1] + stats[2])
                else:
                    top = functools.reduce(jnp.maximum, stats)
                    ws = [jnp.exp(l - top) for l in stats]
                    num = ws[0] * accs[0] + ws[1] * accs[1] + ws[2] * accs[2]
                    merged = num / (ws[0] + ws[1] + ws[2])
                out_ref[pl.ds(r0, ATT_BLOCK), h * HEAD_DIM:(h + 1) * HEAD_DIM] = merged.astype(BF16)
            return carry

        lax.fori_loop(0, SPAN // ATT_BLOCK, block, 0, unroll=8 if fast else 2)


def _attn_kernel(bound_ref, *refs):
    q_refs = refs[0:N_GROUPS]
    kv_refs = refs[N_GROUPS:5 * N_GROUPS]
    out_ref = refs[5 * N_GROUPS]
    scr = refs[5 * N_GROUPS + 1:]
    acc_scrs, stat_scrs, bias_scr = scr[:N_GROUPS - 1], scr[N_GROUPS - 1:2 * (N_GROUPS - 1)], scr[-1]
    first_span = pl.program_id(1) == 0
    bound = bound_ref[0]
    small = bound < SOFTMAX_SHIFT_LIMIT

    @pl.when(small)
    def _():
        _attn_span(q_refs, kv_refs, out_ref, acc_scrs, stat_scrs, bias_scr, first_span, bound)

    @pl.when(jnp.logical_not(small))
    def _():
        _attn_span(q_refs, kv_refs, out_ref, acc_scrs, stat_scrs, bias_scr, first_span, None)


def _attention(q, kv, bound):
    B, S, _ = q.shape
    qw = Q_PER_KV * HEAD_DIM
    prev = lambda n: jnp.maximum(n - 1, 0)
    in_specs = [pl.BlockSpec(memory_space=pltpu.SMEM)]
    for gi in range(N_GROUPS):
        in_specs.append(pl.BlockSpec((None, SPAN, qw), lambda b, n, h, gi=gi: (b, n, gi * KV_HEADS + h)))
    for gi in range(N_GROUPS):
        kcol = lambda h, gi=gi: gi * 2 * KV_HEADS + h
        vcol = lambda h, gi=gi: gi * 2 * KV_HEADS + KV_HEADS + h
        for col in (kcol, vcol):
            in_specs.append(pl.BlockSpec((None, SPAN, HEAD_DIM),
                                         lambda b, n, h, col=col: (b, n, col(h))))
            in_specs.append(pl.BlockSpec((None, SPAN, HEAD_DIM),
                                         lambda b, n, h, col=col: (b, prev(n), col(h))))
    merge_scratch = lambda: [
        pltpu.VMEM((Q_PER_KV, SPAN // d * _merge_pitch(d), HEAD_DIM), F32)
        for _, d in DILATION_GROUPS[1:]]
    return pl.pallas_call(
        _attn_kernel,
        grid=(B, S // SPAN, KV_HEADS),
        in_specs=in_specs,
        out_specs=pl.BlockSpec((None, SPAN, qw), lambda b, n, h: (b, n, h)),
        out_shape=jax.ShapeDtypeStruct((B, S, Q_HEADS * HEAD_DIM), BF16),
        scratch_shapes=merge_scratch() + merge_scratch() + [
            pltpu.VMEM((2, Q_PER_KV * ATT_BLOCK, 2 * ATT_BLOCK), F32)],
        compiler_params=pltpu.CompilerParams(
            dimension_semantics=("arbitrary",) * 3, vmem_limit_bytes=VMEM_LIMIT),
        name="attn",
    )(bound, *([q] * N_GROUPS), *([kv] * (4 * N_GROUPS)))


def kernel(x, positions, a_norm, a_w_in, a_conv_w, a_conv_b, a_gate_a_w, a_gate_a_b, a_gate_x_w, a_gate_x_b, a_lambda, a_w_out, a_ffn_norm, a_ffn_w_in, a_ffn_w_out, kv_norm, w_kv, k_norm, b_norm, b_w_q, b_q_norm, b_w_o, b_ffn_norm, b_ffn_w_in, b_ffn_w_out):
    assert x.shape[1] % SPAN == 0 and x.shape[2] == D_MODEL
    assert a_norm.shape[0] == 1 and b_norm.shape[0] == 1
    hidden = a_ffn_w_out.shape[1]
    h = x
    w_gates = (0.5 * jnp.concatenate([a_gate_a_w[0], a_gate_x_w[0]], axis=-1)).astype(BF16)
    w_in = jnp.concatenate([0.5 * a_w_in[0][:, :D_MODEL], a_w_in[0][:, D_MODEL:]], axis=1).astype(BF16)
    h = _layer_a(h, a_norm[0], w_in, a_conv_w[0], a_conv_b[0], w_gates,
                 0.5 * a_gate_a_b[0], 0.5 * a_gate_x_b[0], a_lambda[0], a_w_out[0].astype(BF16),
                 a_ffn_norm[0], (0.5 * a_ffn_w_in[0][:, :hidden]).astype(BF16),
                 a_ffn_w_in[0][:, hidden:].astype(BF16), a_ffn_w_out[0].astype(BF16))
    cos_t, sin_t = _rope_tables(positions)
    q, kv = _qkv(h, b_norm[0], kv_norm, b_w_q[0].astype(BF16), w_kv.astype(BF16),
                 b_q_norm[0], k_norm, cos_t, sin_t)
    gain_prod = jnp.max(jnp.abs(b_q_norm[0]), axis=-1) * jnp.max(jnp.abs(k_norm), axis=-1)
    bound = (1.02 * HEAD_DIM ** 0.5) * jnp.max(gain_prod).reshape(1)
    att = _attention(q, kv, bound)
    return _layer_b_out(h, att, b_w_o[0].astype(BF16), b_ffn_norm[0],
                        (0.5 * b_ffn_w_in[0][:, :hidden]).astype(BF16),
                        b_ffn_w_in[0][:, hidden:].astype(BF16), b_ffn_w_out[0].astype(BF16))
```

```python
import functools

import jax
import jax.numpy as jnp
from jax import lax
from jax.experimental import pallas as pl
from jax.experimental.pallas import tpu as pltpu

D_MODEL = 1024
LRU_BLOCKS = 8
LRU_BLOCK_W = D_MODEL // LRU_BLOCKS
CONV_WIDTH = 4
LRU_C = 8.0
HEAD_DIM = 128
Q_HEADS = D_MODEL // HEAD_DIM
KV_HEADS = 2
Q_PER_KV = Q_HEADS // KV_HEADS
DILATION_GROUPS = ((128, 1), (512, 4), (2048, 16))
N_GROUPS = len(DILATION_GROUPS)
ATT_BLOCK = 128
ROPE_DIM = HEAD_DIM // 4
ROPE_HALF = ROPE_DIM // 2
ROPE_PACK = 128 // ROPE_DIM
ROPE_THETA = 500000.0
EPS = 1e-6
NEG_INF = -1e30
GELU_C0 = 0.7978845608028654
GELU_C1 = GELU_C0 * 0.044715
F32_TINY = 1.1754943508222875e-38
SOFTMAX_SHIFT_LIMIT = 40.0

LANES = 128
SUBLANES = 8
N_SLABS = D_MODEL // LANES
SPAN = ATT_BLOCK * max(d for _, d in DILATION_GROUPS)
PROJ_CHUNK = 256
MXU_TILE = 256
HEAD_PAIR = 2 * HEAD_DIM
assert HEAD_PAIR == MXU_TILE and KV_HEADS * HEAD_DIM == HEAD_PAIR
OLD, NEW = 0, 1
LAYER_A_TILE = 256
FFN_TILE = 512
VMEM_LIMIT = 56 * 1024 * 1024

F32 = jnp.float32
BF16 = jnp.bfloat16


def _rms_rows(x):
    return x * lax.rsqrt(jnp.mean(x * x, axis=-1, keepdims=True) + EPS)


def _sigmoid_of_half(hx):
    return 0.5 * jnp.tanh(hx) + 0.5


def _silu_of_half(hx):
    return hx + hx * jnp.tanh(hx)


def _gelu_of_half(hx):
    return hx + hx * jnp.tanh(hx * (2.0 * GELU_C0 + 8.0 * GELU_C1 * (hx * hx)))


def _dot(a, b):
    return jnp.dot(a, b, preferred_element_type=F32)


def _const_spec(shape):
    nd = len(shape)
    return pl.BlockSpec(shape, lambda *_: (0,) * nd, pipeline_mode=pl.Buffered(1))


def _swiglu_residual(x, g_ref, wgu_ref, wd_ref):
    hidden = wd_ref.shape[0]
    hn = (_rms_rows(x) * g_ref[...]).astype(BF16)
    half_g = _dot(hn, wgu_ref[:, :hidden])
    uu = _dot(hn, wgu_ref[:, hidden:])
    return x + _dot((_silu_of_half(half_g) * uu).astype(BF16), wd_ref[...])


def _mixer_chunk(half_y, xb, c, cw_ref, cb_ref, wgates_ref, ba_ref, bx_ref, decay, xpad_ref,
                 carry_ref, y_ref):
    tm = xb.shape[0]
    cols = slice(c * MXU_TILE, (c + 1) * MXU_TILE)
    gate = _gelu_of_half(half_y)
    xpad_ref[SUBLANES:, cols] = xb
    xc = cb_ref[:, cols]
    for k in range(CONV_WIDTH):
        off = SUBLANES - (CONV_WIDTH - 1) + k
        xc = xc + cw_ref[k:k + 1, cols] * xpad_ref[off:off + tm, cols]

    xcb = xc.astype(BF16)
    ra, rx = [], []
    for b in range(MXU_TILE // LRU_BLOCK_W):
        r = _dot(xcb[:, b * LRU_BLOCK_W:(b + 1) * LRU_BLOCK_W],
                 wgates_ref[c * (MXU_TILE // LRU_BLOCK_W) + b])
        ra.append(r[:, :LRU_BLOCK_W])
        rx.append(r[:, LRU_BLOCK_W:])
    r_gate = _sigmoid_of_half(jnp.concatenate(ra, axis=1) + ba_ref[:, cols])
    i_gate = _sigmoid_of_half(jnp.concatenate(rx, axis=1) + bx_ref[:, cols])
    log_a = decay[:, cols] * r_gate
    a_all = jnp.exp(log_a)
    one_minus_a2 = -jnp.tanh(log_a) * (1.0 + a_all * a_all)
    u_all = (one_minus_a2 * lax.rsqrt(jnp.maximum(one_minus_a2, F32_TINY))) * (i_gate * xc)

    row = lax.broadcasted_iota(jnp.int32, (SUBLANES, MXU_TILE), 0)
    hc = carry_ref[:, cols]
    groups = []
    for gi in range(tm // SUBLANES):
        rows = slice(gi * SUBLANES, (gi + 1) * SUBLANES)
        a, u = a_all[rows], u_all[rows]
        for k in (1, 2, 4):
            keep = row >= k
            u = u + jnp.where(keep, a * pltpu.roll(u, k, 0), 0.0)
            a = a * jnp.where(keep, pltpu.roll(a, k, 0), 1.0)
        hg = u + a * hc
        groups.append(hg)
        hc = jnp.broadcast_to(hg[SUBLANES - 1:SUBLANES, :], (SUBLANES, MXU_TILE))
    carry_ref[:, cols] = hc
    y_ref[:, cols] = (gate * jnp.concatenate(groups, axis=0)).astype(BF16)


def _layer_a_kernel(hcur_ref, hprev_ref, g_ref, win_ref, cw_ref, cb_ref, wgates_ref, ba_ref, bx_ref,
                    lam_ref, wout_ref, gf_ref, wgu_ref, wd_ref, o_ref,
                    y_ref, xpad_ref, carry_ref, h1_ref, act_ref, *, tiles_per_seq):
    i = pl.program_id(0)
    j = i % tiles_per_seq
    tm = hcur_ref.shape[0]

    @pl.when(i == 0)
    def _():
        for ref in (y_ref, h1_ref, act_ref):
            ref[OLD] = jnp.zeros(ref.shape[1:], ref.dtype)

    @pl.when(i > 0)
    def _():
        for ref in (y_ref, h1_ref, act_ref):
            ref[OLD] = ref[NEW]

    @pl.when(j == 0)
    def _():
        xpad_ref[0:SUBLANES, :] = jnp.zeros((SUBLANES, D_MODEL), F32)
        carry_ref[...] = jnp.zeros_like(carry_ref)

    @pl.when(j > 0)
    def _():
        xpad_ref[0:SUBLANES, :] = xpad_ref[tm:tm + SUBLANES, :]

    h1 = hprev_ref[...] + _dot(y_ref[OLD], wout_ref[...])
    hn_ffn = (_rms_rows(h1) * gf_ref[...]).astype(BF16)
    hn_mix = (_rms_rows(hcur_ref[...]) * g_ref[...]).astype(BF16)
    proj = _dot(hn_mix, win_ref[...])
    z = -lam_ref[...]
    decay = (-LRU_C) * (jnp.maximum(z, 0.0) + jnp.log1p(jnp.exp(-jnp.abs(z))))

    n_mix = D_MODEL // MXU_TILE
    hidden = wd_ref.shape[0]
    n_ffn = hidden // MXU_TILE
    gg, uu = [], []
    for c in range(n_mix):
        hid = slice(c * n_ffn // n_mix * MXU_TILE, (c + 1) * n_ffn // n_mix * MXU_TILE)
        gg.append(_dot(hn_ffn, wgu_ref[:, hid]))
        uu.append(_dot(hn_ffn, wgu_ref[:, hidden + hid.start:hidden + hid.stop]))
        cols = slice(c * MXU_TILE, (c + 1) * MXU_TILE)
        _mixer_chunk(proj[:, cols], proj[:, D_MODEL + c * MXU_TILE:D_MODEL + (c + 1) * MXU_TILE], c,
                     cw_ref, cb_ref, wgates_ref, ba_ref, bx_ref, decay, xpad_ref, carry_ref,
                     y_ref.at[NEW])
        o_ref[:, cols] = h1_ref[OLD, :, cols] + _dot(act_ref[OLD], wd_ref[:, cols])
    h1_ref[NEW] = h1
    act_ref[NEW] = (_silu_of_half(jnp.concatenate(gg, axis=1))
                    * jnp.concatenate(uu, axis=1)).astype(BF16)


def _layer_a(h, g, w_in, conv_w, conv_b, w_gates, b_a, b_x, lam, w_out, g_ffn, w_gate_up, w_down):
    B, S, D = h.shape
    T = B * S
    tm = LAYER_A_TILE
    n_tiles = T // tm
    row = lambda v: v.reshape(1, D)
    lag = lambda k: pl.BlockSpec((tm, D), lambda i: (jnp.clip(i - k, 0, n_tiles - 1), 0))
    h2 = h.reshape(T, D)
    consts = [row(g), w_in, conv_w, row(conv_b), w_gates, row(b_a), row(b_x), row(lam), w_out,
              row(g_ffn), w_gate_up, w_down]
    out = pl.pallas_call(
        functools.partial(_layer_a_kernel, tiles_per_seq=S // tm),
        grid=(n_tiles + 2,),
        in_specs=[lag(0), lag(1)] + [_const_spec(c.shape) for c in consts],
        out_specs=lag(2),
        out_shape=jax.ShapeDtypeStruct((T, D), F32),
        scratch_shapes=[pltpu.VMEM((2, tm, D), BF16), pltpu.VMEM((tm + SUBLANES, D), F32),
                        pltpu.VMEM((SUBLANES, D), F32), pltpu.VMEM((2, tm, D), F32),
                        pltpu.VMEM((2, tm, w_down.shape[0]), BF16)],
        compiler_params=pltpu.CompilerParams(
            dimension_semantics=("arbitrary",), vmem_limit_bytes=VMEM_LIMIT),
        name="layer_a",
    )(h2, h2, *consts)
    return out.reshape(B, S, D)


def _layer_b_out_kernel(h_ref, att_ref, wo_ref, g_ref, wgu_ref, wd_ref, o_ref):
    x = h_ref[...] + _dot(att_ref[...], wo_ref[...])
    o_ref[...] = _swiglu_residual(x, g_ref, wgu_ref, wd_ref)


def _layer_b_out(h, att, w_o, g, w_gate_up, w_down):
    B, S, D = h.shape
    T = B * S
    tm = FFN_TILE
    tile = pl.BlockSpec((tm, D), lambda i: (i, 0))
    consts = [w_o, g.reshape(1, D), w_gate_up, w_down]
    out = pl.pallas_call(
        _layer_b_out_kernel,
        grid=(T // tm,),
        in_specs=[tile, tile] + [_const_spec(c.shape) for c in consts],
        out_specs=tile,
        out_shape=jax.ShapeDtypeStruct((T, D), F32),
        compiler_params=pltpu.CompilerParams(
            dimension_semantics=("arbitrary",), vmem_limit_bytes=VMEM_LIMIT),
        name="ffn_proj",
    )(h.reshape(T, D), att.reshape(T, D), *consts)
    return out.reshape(B, S, D)


def _rope_kernel(pos_ref, invf_ref, sign_ref, cos_ref, sin_ref):
    n = pos_ref.shape[-1]
    q = n // ROPE_PACK
    posb = jnp.broadcast_to(pos_ref[...], (LANES, n)).T
    lane = lax.broadcasted_iota(jnp.int32, (q, LANES), 1)
    packed = posb[0:q]
    for k in range(1, ROPE_PACK):
        packed = jnp.where(lane >= k * ROPE_DIM, posb[k * q:(k + 1) * q], packed)
    ang = packed * invf_ref[...]
    cos_p = jnp.cos(ang)
    sin_p = jnp.sin(ang) * sign_ref[...]
    for k in range(ROPE_PACK):
        shift = (LANES - k * ROPE_DIM) % LANES
        cos_k = pltpu.roll(cos_p, shift, 1) if shift else cos_p
        sin_k = pltpu.roll(sin_p, shift, 1) if shift else sin_p
        cos_ref[k * q:(k + 1) * q, :] = jnp.where(lane < ROPE_DIM, cos_k, 1.0)
        sin_ref[k * q:(k + 1) * q, :] = jnp.where(lane < ROPE_DIM, sin_k, 0.0)


def _rope_tables(positions):
    B, S = positions.shape
    inv_freq = ROPE_THETA ** (-jnp.arange(0, ROPE_DIM, 2, dtype=F32) / ROPE_DIM)
    invf = jnp.tile(jnp.concatenate([inv_freq, inv_freq]), ROPE_PACK)[None]
    sign = jnp.tile(jnp.concatenate([-jnp.ones((ROPE_HALF,), F32), jnp.ones((ROPE_HALF,), F32)]),
                    ROPE_PACK)[None]
    pos = positions.astype(F32).reshape(B, 1, S)
    table = pl.BlockSpec((None, SPAN, LANES), lambda b, n: (b, n, 0))
    return pl.pallas_call(
        _rope_kernel,
        grid=(B, S // SPAN),
        in_specs=[pl.BlockSpec((None, 1, SPAN), lambda b, n: (b, 0, n)),
                  _const_spec((1, LANES)), _const_spec((1, LANES))],
        out_specs=[table, table],
        out_shape=[jax.ShapeDtypeStruct((B, S, LANES), F32)] * 2,
        compiler_params=pltpu.CompilerParams(dimension_semantics=("arbitrary", "arbitrary")),
        name="rope",
    )(pos, invf, sign)


def _pair_ones():
    shape = (HEAD_PAIR, HEAD_PAIR)
    src = lax.broadcasted_iota(jnp.int32, shape, 0) // HEAD_DIM
    dst = lax.broadcasted_iota(jnp.int32, shape, 1) // HEAD_DIM
    return jnp.where(src == dst, 1.0, 0.0).astype(BF16)


def _norm_rope_pair(t, gain2, cos2, sin2, ones, low_half):
    ss = _dot((t * t).astype(BF16), ones)
    z = t * gain2
    parts = []
    for h in range(HEAD_PAIR // HEAD_DIM):
        zh = z[:, h * HEAD_DIM:(h + 1) * HEAD_DIM]
        parts.append(jnp.where(low_half, pltpu.roll(zh, HEAD_DIM - ROPE_HALF, 1),
                               pltpu.roll(zh, ROPE_HALF, 1)))
    partner = jnp.concatenate(parts, axis=1)
    return lax.rsqrt(ss * (1.0 / HEAD_DIM) + EPS) * (z * cos2 + partner * sin2)


def _qkv_group(src_refs, src_dilation, stage_refs, gq, gkv, wq_ref, wkv_ref, qn2, kn2, q_out,
               kv_out, dilation):
    per_phase = SPAN // dilation
    n_pieces = max(1, PROJ_CHUNK // per_phase)
    piece = PROJ_CHUNK // n_pieces
    step = dilation // src_dilation
    ones = _pair_ones()
    rot = lax.broadcasted_iota(jnp.int32, (PROJ_CHUNK, HEAD_DIM), 1) < ROPE_HALF

    def chunk(c, carry):
        row0 = pl.multiple_of(c * PROJ_CHUNK, PROJ_CHUNK)

        def gather(ref):
            parts = []
            for i in range(n_pieces):
                r = row0 + i * piece
                phase, l0 = r // per_phase, r % per_phase
                start = (phase % src_dilation) * (SPAN // src_dilation) + l0 * step + phase // src_dilation
                idx = pl.ds(start, piece, stride=step) if step > 1 else pl.ds(start, piece)
                parts.append(ref[idx, :])
            return parts[0] if n_pieces == 1 else jnp.concatenate(parts, axis=0)

        gathered = [gather(s) for s in src_refs]
        if stage_refs is not None:
            for dst, val in zip(stage_refs, gathered):
                dst[pl.ds(row0, PROJ_CHUNK), :] = val
        y = _rms_rows(jnp.concatenate(gathered[:N_SLABS], axis=1))
        cosv, sinv = gathered[N_SLABS:]
        cos2 = jnp.concatenate([cosv, cosv], axis=1)
        sin2 = jnp.concatenate([sinv, sinv], axis=1)
        rows = pl.ds(row0, PROJ_CHUNK)
        q = _dot((y * gq).astype(BF16), wq_ref[...])
        for i in range(Q_HEADS * HEAD_DIM // HEAD_PAIR):
            cols = slice(i * HEAD_PAIR, (i + 1) * HEAD_PAIR)
            q_out[rows, cols] = _norm_rope_pair(q[:, cols], qn2, cos2, sin2, ones, rot).astype(BF16)
        kv = _dot((y * gkv).astype(BF16), wkv_ref[...])
        kcols = slice(0, HEAD_PAIR)
        kv_out[rows, kcols] = _norm_rope_pair(kv[:, kcols], kn2, cos2, sin2, ones, rot).astype(BF16)
        vcols = slice(HEAD_PAIR, 2 * HEAD_PAIR)
        kv_out[rows, vcols] = kv[:, vcols].astype(BF16)
        return carry

    lax.fori_loop(0, SPAN // PROJ_CHUNK, chunk, 0, unroll=2)


def _staging_plan():
    dils = [d for _, d in DILATION_GROUPS]
    src = [None] * len(dils)
    for gi in range(1, len(dils)):
        if dils[gi - 1] > 1 and dils[gi] % dils[gi - 1] == 0 and _merge_pitch(dils[gi]) != dils[gi]:
            src[gi] = gi - 1
    return [(src[gi], gi in src) for gi in range(len(dils))]


def _qkv_kernel(*refs):
    assert sum(staged for _, staged in _staging_plan()) <= 1
    slab_refs = refs[:N_SLABS]
    gq_ref, gkv_ref, wq_ref, wkv_ref, qn_ref, kn_ref, cos_ref, sin_ref, q_out, kv_out = (
        refs[N_SLABS:N_SLABS + 10])
    stage_refs = refs[N_SLABS + 10:]
    token_order = list(slab_refs) + [cos_ref, sin_ref]
    g = pl.program_id(2)
    for gi, ((_, dilation), (src, staged)) in enumerate(zip(DILATION_GROUPS, _staging_plan())):
        @pl.when(g == gi)
        def _(gi=gi, dilation=dilation, src=src, staged=staged):
            qn = qn_ref[gi:gi + 1, :] * (HEAD_DIM ** -0.5)
            kn = kn_ref[gi:gi + 1, :]
            _qkv_group(token_order if src is None else stage_refs,
                       1 if src is None else DILATION_GROUPS[src][1],
                       stage_refs if staged else None, gq_ref[...], gkv_ref[...], wq_ref, wkv_ref,
                       jnp.concatenate([qn, qn], axis=1), jnp.concatenate([kn, kn], axis=1),
                       q_out, kv_out, dilation)


def _qkv(h, g_q, g_kv, w_q, w_kv, q_norm, k_norm, cos_t, sin_t):
    B, S, D = h.shape
    qw = Q_HEADS * HEAD_DIM
    kvw = 2 * KV_HEADS * HEAD_DIM
    slab = lambda c: pl.BlockSpec((None, SPAN, LANES), lambda b, n, g: (b, n, c))
    table = pl.BlockSpec((None, SPAN, LANES), lambda b, n, g: (b, n, 0))
    const = lambda shape: pl.BlockSpec(shape, lambda b, n, g: (0,) * len(shape))
    return pl.pallas_call(
        _qkv_kernel,
        grid=(B, S // SPAN, N_GROUPS),
        in_specs=[slab(c) for c in range(N_SLABS)] + [
            const((1, D)), const((1, D)),
            pl.BlockSpec((D, qw), lambda b, n, g: (0, g)),
            pl.BlockSpec((D, kvw), lambda b, n, g: (0, g)),
            const(q_norm.shape), const(k_norm.shape), table, table],
        out_specs=[pl.BlockSpec((None, SPAN, qw), lambda b, n, g: (b, n, g)),
                   pl.BlockSpec((None, SPAN, kvw), lambda b, n, g: (b, n, g))],
        out_shape=[jax.ShapeDtypeStruct((B, S, N_GROUPS * qw), BF16),
                   jax.ShapeDtypeStruct((B, S, N_GROUPS * kvw), BF16)],
        scratch_shapes=[pltpu.VMEM((SPAN, LANES), F32) for _ in range(N_SLABS + 2)],
        compiler_params=pltpu.CompilerParams(
            dimension_semantics=("arbitrary",) * 3, vmem_limit_bytes=VMEM_LIMIT),
        name="qkv",
    )(*([h] * N_SLABS), g_q.reshape(1, D), g_kv.reshape(1, D), w_q, w_kv, q_norm, k_norm,
      cos_t, sin_t)


def _merge_pitch(dilation):
    if dilation % (2 * SUBLANES) != 0:
        return dilation
    return dilation + SUBLANES


def _attn_span(q_refs, kv_refs, out_ref, acc_scrs, stat_scrs, bias_scr, first_span, shift):
    fast = shift is not None
    rows = Q_PER_KV * ATT_BLOCK

    qi = lax.broadcasted_iota(jnp.int32, (rows, 2 * ATT_BLOCK), 0) % ATT_BLOCK
    ki = lax.broadcasted_iota(jnp.int32, (rows, 2 * ATT_BLOCK), 1)
    vis_cur = jnp.logical_and(ki >= ATT_BLOCK, ki - ATT_BLOCK <= qi)
    vis_prev = jnp.logical_and(ki < ATT_BLOCK, ki >= qi)
    base = -shift if fast else 0.0
    bias_scr[0] = jnp.where(jnp.logical_or(vis_cur, vis_prev), base, NEG_INF)
    bias_scr[1] = jnp.where(vis_cur, base, NEG_INF)
    ones_cols = jnp.ones((2 * ATT_BLOCK, HEAD_DIM), BF16)

    for gi in reversed(range(N_GROUPS)):
        dilation = DILATION_GROUPS[gi][1]
        per_phase = SPAN // dilation
        nblk = per_phase // ATT_BLOCK
        pitch = _merge_pitch(dilation)
        q_ref = q_refs[gi]
        kc_ref, kp_ref, vc_ref, vp_ref = kv_refs[4 * gi:4 * gi + 4]

        def block(blk, carry, gi=gi, per_phase=per_phase, nblk=nblk, pitch=pitch,
                  q_ref=q_ref, kc_ref=kc_ref, kp_ref=kp_ref, vc_ref=vc_ref, vp_ref=vp_ref):
            r0 = pl.multiple_of(blk * ATT_BLOCK, ATT_BLOCK)
            j = blk % nblk
            phase = blk // nblk
            head_of_phase = j == 0
            q4 = q_ref[pl.ds(r0, ATT_BLOCK), :]
            qs = jnp.concatenate(
                [q4[:, h * HEAD_DIM:(h + 1) * HEAD_DIM] for h in range(Q_PER_KV)], axis=0)
            r_in = pl.multiple_of(jnp.maximum(r0 - ATT_BLOCK, 0), ATT_BLOCK)
            r_out = pl.multiple_of(
                jnp.where(head_of_phase, r0 + per_phase - ATT_BLOCK, 0), ATT_BLOCK)
            k_prev = jnp.where(head_of_phase, kp_ref[pl.ds(r_out, ATT_BLOCK), :],
                               kc_ref[pl.ds(r_in, ATT_BLOCK), :])
            v_prev = jnp.where(head_of_phase, vp_ref[pl.ds(r_out, ATT_BLOCK), :],
                               vc_ref[pl.ds(r_in, ATT_BLOCK), :])
            kcat = jnp.concatenate([k_prev, kc_ref[pl.ds(r0, ATT_BLOCK), :]], axis=0)
            vcat = jnp.concatenate([v_prev, vc_ref[pl.ds(r0, ATT_BLOCK), :]], axis=0)
            s = lax.dot_general(qs, kcat, (((1,), (1,)), ((), ())), preferred_element_type=F32)
            no_prev = jnp.logical_and(head_of_phase, first_span)
            s = s + jnp.where(no_prev, bias_scr[1], bias_scr[0])
            if fast:
                av = _dot(jnp.exp(s).astype(BF16), jnp.concatenate([vcat, ones_cols], axis=1))
                acc, stat = av[:, :HEAD_DIM], av[:, HEAD_DIM:]
            else:
                m = jnp.max(s, axis=-1, keepdims=True)
                p = jnp.exp(s - m)
                den = jnp.sum(p, axis=-1, keepdims=True)
                acc = _dot(p.astype(BF16), vcat) * (1.0 / den)
                stat = jnp.broadcast_to(m + jnp.log(den), (rows, HEAD_DIM))
            for h in range(Q_PER_KV):
                hrows = slice(h * ATT_BLOCK, (h + 1) * ATT_BLOCK)
                if gi > 0:
                    tok = pl.ds(j * (ATT_BLOCK * pitch) + phase, ATT_BLOCK, stride=pitch)
                    acc_scrs[gi - 1][h, tok, :] = acc[hrows]
                    stat_scrs[gi - 1][h, tok, :] = stat[hrows]
                    continue
                accs, stats = [acc[hrows]], [stat[hrows]]
                for og in range(1, N_GROUPS):
                    od = DILATION_GROUPS[og][1]
                    op = _merge_pitch(od)
                    n_parts, n = (1, ATT_BLOCK) if op == od else (ATT_BLOCK // od, od)
                    a_parts, s_parts = [], []
                    for i in range(n_parts):
                        src = pl.ds(pl.multiple_of((r0 // od + i) * op, SUBLANES), n)
                        a_parts.append(acc_scrs[og - 1][h, src, :])
                        s_parts.append(stat_scrs[og - 1][h, src, :])
                    accs.append(jnp.concatenate(a_parts, axis=0))
                    stats.append(jnp.concatenate(s_parts, axis=0))
                if fast:
                    merged = (accs[0] + accs[1] + accs[2]) / (stats[0] + stats[1] + stats[2])
                else:
                    top = functools.reduce(jnp.maximum, stats)
                    ws = [jnp.exp(l - top) for l in stats]
                    num = ws[0] * accs[0] + ws[1] * accs[1] + ws[2] * accs[2]
                    merged = num / (ws[0] + ws[1] + ws[2])
                out_ref[pl.ds(r0, ATT_BLOCK), h * HEAD_DIM:(h + 1) * HEAD_DIM] = merged.astype(BF16)
            return carry

        lax.fori_loop(0, SPAN // ATT_BLOCK, block, 0, unroll=8 if fast else 2)


def _attn_kernel(bound_ref, *refs):
    q_refs = refs[0:N_GROUPS]
    kv_refs = refs[N_GROUPS:5 * N_GROUPS]
    out_ref = refs[5 * N_GROUPS]
    scr = refs[5 * N_GROUPS + 1:]
    acc_scrs, stat_scrs, bias_scr = scr[:N_GROUPS - 1], scr[N_GROUPS - 1:2 * (N_GROUPS - 1)], scr[-1]
    first_span = pl.program_id(1) == 0
    bound = bound_ref[0]
    small = bound < SOFTMAX_SHIFT_LIMIT

    @pl.when(small)
    def _():
        _attn_span(q_refs, kv_refs, out_ref, acc_scrs, stat_scrs, bias_scr, first_span, bound)

    @pl.when(jnp.logical_not(small))
    def _():
        _attn_span(q_refs, kv_refs, out_ref, acc_scrs, stat_scrs, bias_scr, first_span, None)


def _attention(q, kv, bound):
    B, S, _ = q.shape
    qw = Q_PER_KV * HEAD_DIM
    prev = lambda n: jnp.maximum(n - 1, 0)
    in_specs = [pl.BlockSpec(memory_space=pltpu.SMEM)]
    for gi in range(N_GROUPS):
        in_specs.append(pl.BlockSpec((None, SPAN, qw), lambda b, n, h, gi=gi: (b, n, gi * KV_HEADS + h)))
    for gi in range(N_GROUPS):
        kcol = lambda h, gi=gi: gi * 2 * KV_HEADS + h
        vcol = lambda h, gi=gi: gi * 2 * KV_HEADS + KV_HEADS + h
        for col in (kcol, vcol):
            in_specs.append(pl.BlockSpec((None, SPAN, HEAD_DIM),
                                         lambda b, n, h, col=col: (b, n, col(h))))
            in_specs.append(pl.BlockSpec((None, SPAN, HEAD_DIM),
                                         lambda b, n, h, col=col: (b, prev(n), col(h))))
    merge_scratch = lambda: [
        pltpu.VMEM((Q_PER_KV, SPAN // d * _merge_pitch(d), HEAD_DIM), F32)
        for _, d in DILATION_GROUPS[1:]]
    return pl.pallas_call(
        _attn_kernel,
        grid=(B, S // SPAN, KV_HEADS),
        in_specs=in_specs,
        out_specs=pl.BlockSpec((None, SPAN, qw), lambda b, n, h: (b, n, h)),
        out_shape=jax.ShapeDtypeStruct((B, S, Q_HEADS * HEAD_DIM), BF16),
        scratch_shapes=merge_scratch() + merge_scratch() + [
            pltpu.VMEM((2, Q_PER_KV * ATT_BLOCK, 2 * ATT_BLOCK), F32)],
        compiler_params=pltpu.CompilerParams(
            dimension_semantics=("arbitrary",) * 3, vmem_limit_bytes=VMEM_LIMIT),
        name="attn",
    )(bound, *([q] * N_GROUPS), *([kv] * (4 * N_GROUPS)))


def kernel(x, positions, a_norm, a_w_in, a_conv_w, a_conv_b, a_gate_a_w, a_gate_a_b, a_gate_x_w, a_gate_x_b, a_lambda, a_w_out, a_ffn_norm, a_ffn_w_in, a_ffn_w_out, kv_norm, w_kv, k_norm, b_norm, b_w_q, b_q_norm, b_w_o, b_ffn_norm, b_ffn_w_in, b_ffn_w_out):
    assert x.shape[1] % SPAN == 0 and x.shape[2] == D_MODEL
    assert a_norm.shape[0] == 1 and b_norm.shape[0] == 1
    h = x
    def first_half_halved(w):
        n = w.shape[1] // 2
        return (w * jnp.concatenate([jnp.full((n,), 0.5, F32), jnp.ones((n,), F32)])).astype(BF16)

    w_gates = (0.5 * jnp.concatenate([a_gate_a_w[0], a_gate_x_w[0]], axis=-1)).astype(BF16)
    h = _layer_a(h, a_norm[0], first_half_halved(a_w_in[0]), a_conv_w[0], a_conv_b[0], w_gates,
                 0.5 * a_gate_a_b[0], 0.5 * a_gate_x_b[0], a_lambda[0], a_w_out[0].astype(BF16),
                 a_ffn_norm[0], first_half_halved(a_ffn_w_in[0]), a_ffn_w_out[0].astype(BF16))
    cos_t, sin_t = _rope_tables(positions)
    q, kv = _qkv(h, b_norm[0], kv_norm, b_w_q[0].astype(BF16), w_kv.astype(BF16),
                 b_q_norm[0], k_norm, cos_t, sin_t)
    gain_prod = jnp.max(jnp.abs(b_q_norm[0]), axis=-1) * jnp.max(jnp.abs(k_norm), axis=-1)
    bound = (1.02 * HEAD_DIM ** 0.5) * jnp.max(gain_prod).reshape(1)
    att = _attention(q, kv, bound)
    return _layer_b_out(h, att, b_w_o[0].astype(BF16), b_ffn_norm[0],
                        first_half_halved(b_ffn_w_in[0]), b_ffn_w_out[0].astype(BF16))
```
